```python
import jax, jax.numpy as jnp
from jax import lax
import numpy as np

D_MODEL = 1024
BATCH = 16
SEQ = 2048
DEPTH = 1

N_META = 16
D_MIX = D_MODEL
HEAD_DIM = 64
D_ATTN = D_MIX // 2
N_ATTN_HEADS = D_ATTN // HEAD_DIM
D_CONV = D_MIX - D_ATTN
N_CONV_GROUPS = D_CONV // HEAD_DIM
CONV_WIDTH = 3
D_FF = 4 * D_MODEL
Q_BLOCK = 128
EPS = 1e-5
NEG_INF = -1e30
D_IN_PROJ = 3 * D_ATTN + N_ATTN_HEADS + 3 * D_CONV

kernel_name = "hymba_fox_shortconv_block"


def rms_norm(x, g):
    xf = x.astype(jnp.float32)
    y = xf * lax.rsqrt(jnp.mean(xf * xf, axis=-1, keepdims=True) + EPS)
    return (y * g.astype(jnp.float32)).astype(x.dtype)


def head_rms_norm(y, n_groups, g):
    b, l, c = y.shape
    yf = y.astype(jnp.float32).reshape(b, l, n_groups, c // n_groups)
    yf = yf * lax.rsqrt(jnp.mean(yf * yf, axis=-1, keepdims=True) + EPS)
    return (yf.reshape(b, l, c) * g.astype(jnp.float32)).astype(y.dtype)


def forgetting_attention(q, k, v, log_f):
    L = q.shape[1]
    cum = jnp.cumsum(log_f.astype(jnp.float32), axis=1).transpose(0, 2, 1)
    scale = HEAD_DIM ** -0.5
    n_blocks = (L - N_META) // Q_BLOCK
    bounds = [(0, N_META)] + [(N_META + i * Q_BLOCK, N_META + (i + 1) * Q_BLOCK)
                              for i in range(n_blocks)]
    outs = []
    for lo, hi in bounds:
        qb, kb, vb = q[:, lo:hi], k[:, :hi], v[:, :hi]
        s = jnp.einsum('bqhd,bkhd->bhqk', qb, kb,
                       preferred_element_type=jnp.float32) * scale
        decay = cum[:, :, lo:hi, None] - cum[:, :, None, :hi]
        causal = jnp.arange(hi)[None, :] <= jnp.arange(lo, hi)[:, None]
        s = jnp.where(causal, s + decay, NEG_INF)
        p = jax.nn.softmax(s, axis=-1)
        outs.append(jnp.einsum('bhqk,bkhd->bqhd', p.astype(vb.dtype), vb))
    return jnp.concatenate(outs, axis=1)


def causal_depthwise_conv(u, w):
    L = u.shape[1]
    up = jnp.pad(u, ((0, 0), (CONV_WIDTH - 1, 0), (0, 0)))
    y = w[0] * up[:, 0:L]
    for kk in range(1, CONV_WIDTH):
        y = y + w[kk] * up[:, kk:kk + L]
    return y


def setup_inputs(seed: int = 0) -> dict:
    key = jax.random.key(seed)
    ks = jax.random.split(key, 14)
    f32 = jnp.float32
    x = jax.random.normal(ks[0], (BATCH, SEQ, D_MODEL), f32)
    meta_tokens = jax.random.normal(ks[1], (N_META, D_MODEL), f32)
    norm_mix_g = 1.0 + 0.02 * jax.random.normal(ks[2], (DEPTH, D_MODEL), f32)
    w_in = jax.random.normal(ks[3], (DEPTH, D_MODEL, D_IN_PROJ), f32) * D_MODEL ** -0.5
    b_f = (jnp.linspace(1.0, 6.0, N_ATTN_HEADS, dtype=f32)[None, :]
           + 0.1 * jax.random.normal(ks[4], (DEPTH, N_ATTN_HEADS), f32))
    conv_w = jax.random.normal(ks[5], (DEPTH, CONV_WIDTH, D_CONV), f32) * CONV_WIDTH ** -0.5
    out_norm_g = 1.0 + 0.02 * jax.random.normal(ks[6], (DEPTH, D_MIX), f32)
    w_out = jax.random.normal(ks[7], (DEPTH, D_MIX, D_MODEL), f32) * D_MIX ** -0.5
    norm_mlp_g = 1.0 + 0.02 * jax.random.normal(ks[8], (DEPTH, D_MODEL), f32)
    w_ff1 = jax.random.normal(ks[9], (DEPTH, D_MODEL, D_FF), f32) * D_MODEL ** -0.5
    w_ff2 = jax.random.normal(ks[10], (DEPTH, D_FF, D_MODEL), f32) * D_FF ** -0.5
    final_norm_g = 1.0 + 0.02 * jax.random.normal(ks[11], (D_MODEL,), f32)
    return {"x": x, "meta_tokens": meta_tokens, "norm_mix_g": norm_mix_g, "w_in": w_in,
            "b_f": b_f, "conv_w": conv_w, "out_norm_g": out_norm_g, "w_out": w_out,
            "norm_mlp_g": norm_mlp_g, "w_ff1": w_ff1, "w_ff2": w_ff2,
            "final_norm_g": final_norm_g}


def reference(x, meta_tokens, norm_mix_g, w_in, b_f, conv_w, out_norm_g, w_out,
              norm_mlp_g, w_ff1, w_ff2, final_norm_g):
    B = x.shape[0]
    meta = jnp.broadcast_to(meta_tokens.astype(x.dtype)[None], (B, N_META, D_MODEL))
    h = jnp.concatenate([meta, x], axis=1)
    L = h.shape[1]
    split_at = [D_ATTN, 2 * D_ATTN, 3 * D_ATTN, 3 * D_ATTN + N_ATTN_HEADS,
                3 * D_ATTN + N_ATTN_HEADS + D_CONV, 3 * D_ATTN + N_ATTN_HEADS + 2 * D_CONV]
    for layer in range(DEPTH):
        xn = rms_norm(h, norm_mix_g[layer])
        proj = jnp.einsum('bld,de->ble', xn, w_in[layer])
        q, k, v, f_logit, b_gate, c_gate, u = jnp.split(proj, split_at, axis=-1)
        log_f = jax.nn.log_sigmoid(f_logit.astype(jnp.float32) + b_f[layer].astype(jnp.float32))
        hs = (B, L, N_ATTN_HEADS, HEAD_DIM)
        y_attn = forgetting_attention(q.reshape(hs), k.reshape(hs), v.reshape(hs), log_f)
        y_attn = y_attn.reshape(B, L, D_ATTN)
        y_conv = b_gate * causal_depthwise_conv(c_gate * u, conv_w[layer].astype(u.dtype))
        y = jnp.concatenate([head_rms_norm(y_attn, N_ATTN_HEADS, out_norm_g[layer, :D_ATTN]),
                             head_rms_norm(y_conv, N_CONV_GROUPS, out_norm_g[layer, D_ATTN:])],
                            axis=-1)
        h = h + jnp.einsum('ble,ed->bld', y, w_out[layer])
        hn = rms_norm(h, norm_mlp_g[layer])
        a = jnp.square(jax.nn.relu(jnp.einsum('bld,df->blf', hn, w_ff1[layer])))
        h = h + jnp.einsum('blf,fd->bld', a, w_ff2[layer])
    h = rms_norm(h, final_norm_g)
    return h[:, N_META:]
```

```python
import functools

import jax
import jax.numpy as jnp
from jax import lax
from jax.experimental import pallas as pl
from jax.experimental.pallas import tpu as pltpu

D_MODEL = 1024
N_META = 16
HEAD_DIM = 64
D_ATTN = 512
N_HEADS = D_ATTN // HEAD_DIM
D_CONV = 512
CONV_WIDTH = 3
D_FF = 4096
EPS = 1e-5
MASKED = 1e30

LANES = 128
SUBLANES = 8
PAIR = 2 * HEAD_DIM
N_PAIRS = N_HEADS // 2
META_PAD = LANES

PROJ_TILE = 512
ATTN_TILE = 256
MLP_TILE = 512
FF_CHUNK = 1024

BF16 = jnp.bfloat16
F32 = jnp.float32


def _dot(a, b):
    return jnp.dot(a, b, preferred_element_type=F32)


def _dot_nt(a, b):
    return lax.dot_general(a, b, (((1,), (1,)), ((), ())), preferred_element_type=F32)


def _rms_norm(x, g):
    ms = jnp.mean(x * x, axis=-1, keepdims=True)
    return x * lax.rsqrt(ms + EPS) * g


def _split_bf16(x, parts):
    out = []
    r = x
    for _ in range(parts):
        t = r.astype(BF16)
        out.append(t)
        r = r - t.astype(F32)
    return out


def _proj_kernel(x_ref, gmix_ref, wtok_ref, wkf_ref, bf_ref, convw_ref, gconv_ref, gmat_ref, tri_ref,
                 cum0_ref, tail0_ref, *refs, tile, meta):
    if meta:
        kt_ref, v_ref, cum_ref, tail_ref, cu_scr, carry_scr = refs
    else:
        q_ref, kt_ref, v_ref, cum_ref, yc_ref, cu_scr, carry_scr = refs

    @pl.when(pl.program_id(1) == 0)
    def _():
        carry_scr[...] = jnp.broadcast_to(cum0_ref[:, N_META - 1:N_META], carry_scr.shape)
        cu_scr[0:SUBLANES, :] = tail0_ref[...]

    xn = _rms_norm(x_ref[0], gmix_ref[...]).astype(BF16)

    kf = _dot_nt(wkf_ref[...], xn)
    kt_ref[0] = kf[0:D_ATTN].astype(BF16)
    z = kf[D_ATTN:D_ATTN + N_HEADS] + bf_ref[...]
    log_f = jnp.minimum(z, 0.0) - jnp.log1p(jnp.exp(-jnp.abs(z)))
    terms = [t.astype(F32) for t in _split_bf16(log_f, 3)]
    stacked = jnp.concatenate(terms + [jnp.zeros_like(log_f)], axis=0).astype(BF16)
    sums = _dot(stacked, tri_ref[...])
    cum = (sums[0:N_HEADS] + sums[N_HEADS:2 * N_HEADS] + sums[2 * N_HEADS:3 * N_HEADS]
           + carry_scr[:, 0:1])
    carry_scr[...] = jnp.broadcast_to(cum[:, tile - 1:tile], carry_scr.shape)
    if meta:
        lane = lax.broadcasted_iota(jnp.int32, cum.shape, 1)
        cum = jnp.where(lane < N_META, cum, MASKED)
    cum_ref[0] = cum

    if not meta:
        q_ref[0] = _dot(xn, wtok_ref[:, 0:D_ATTN]).astype(BF16)
    v_ref[0] = _dot(xn, wtok_ref[:, D_ATTN:2 * D_ATTN]).astype(BF16)

    c0 = 2 * D_ATTN
    cu = _dot(xn, wtok_ref[:, c0 + D_CONV:c0 + 2 * D_CONV]) * _dot(xn, wtok_ref[:, c0 + 2 * D_CONV:c0 + 3 * D_CONV])
    cu_scr[SUBLANES:SUBLANES + tile, :] = cu
    if meta:
        tail_ref[...] = cu_scr[N_META:N_META + SUBLANES, :]
        return
    cu1 = cu_scr[SUBLANES - 1:SUBLANES - 1 + tile, :]
    cu2 = cu_scr[SUBLANES - 2:SUBLANES - 2 + tile, :]
    conv = convw_ref[0:1, :] * cu2 + convw_ref[1:2, :] * cu1 + convw_ref[2:3, :] * cu
    cu_scr[0:SUBLANES, :] = cu[tile - SUBLANES:tile, :]
    yc = _dot(xn, wtok_ref[:, c0:c0 + D_CONV]) * conv
    sq_hi, sq_lo = _split_bf16(yc * yc, 2)
    ms = _dot(sq_hi, gmat_ref[...]) + _dot(sq_lo, gmat_ref[...])
    yc_ref[0] = (yc * lax.rsqrt(ms + EPS) * gconv_ref[...]).astype(BF16)


def _projection(x, gmix, wtok, wkf, bf_col, convw, gconv, gmat, cum0, tail0, *, tile, meta):
    nb, seq, _ = x.shape
    nt = seq // tile
    tri = jnp.triu(jnp.ones((tile, tile), F32)).astype(BF16)
    const = lambda shape: pl.BlockSpec(shape, lambda b, t: (0,) * len(shape))
    in_specs = [
        pl.BlockSpec((1, tile, D_MODEL), lambda b, t: (b, t, 0)),
        const(gmix.shape), const(wtok.shape), const(wkf.shape), const(bf_col.shape), const(convw.shape),
        const(gconv.shape), const(gmat.shape), const(tri.shape), const(cum0.shape), const(tail0.shape),
    ]
    tok_spec = pl.BlockSpec((1, tile, D_ATTN), lambda b, t: (b, t, 0))
    kt_spec = pl.BlockSpec((1, D_ATTN, tile), lambda b, t: (b, 0, t))
    cum_spec = pl.BlockSpec((1, N_HEADS, tile), lambda b, t: (b, 0, t))
    tok_shape = jax.ShapeDtypeStruct((nb, seq, D_ATTN), BF16)
    kt_shape = jax.ShapeDtypeStruct((nb, D_ATTN, seq), BF16)
    cum_shape = jax.ShapeDtypeStruct((nb, N_HEADS, seq), F32)
    if meta:
        out_specs = [kt_spec, tok_spec, cum_spec, const((SUBLANES, D_CONV))]
        out_shape = [kt_shape, tok_shape, cum_shape, jax.ShapeDtypeStruct((SUBLANES, D_CONV), F32)]
    else:
        out_specs = [tok_spec, kt_spec, tok_spec, cum_spec, tok_spec]
        out_shape = [tok_shape, kt_shape, tok_shape, cum_shape, tok_shape]
    return pl.pallas_call(
        functools.partial(_proj_kernel, tile=tile, meta=meta),
        grid=(nb, nt),
        in_specs=in_specs,
        out_specs=out_specs,
        out_shape=out_shape,
        scratch_shapes=[pltpu.VMEM((tile + SUBLANES, D_CONV), F32), pltpu.VMEM((N_HEADS, LANES), F32)],
        compiler_params=pltpu.CompilerParams(
            dimension_semantics=("arbitrary", "arbitrary"), vmem_limit_bytes=48 * 1024 * 1024),
        name="proj_meta" if meta else "proj",
    )(x, gmix, wtok, wkf, bf_col, convw, gconv, gmat, tri, cum0, tail0)


def _attn_kernel(q_ref, kt_ref, v_ref, cum_ref, ktm_ref, vm_ref, cumm_ref, g_ref, o_ref):
    i = pl.program_id(2)
    t = ATTN_TILE
    q = q_ref[0]
    lane = lax.broadcasted_iota(jnp.int32, (1, PAIR), 1)
    ones = jnp.ones((t, LANES), BF16)
    row = lax.broadcasted_iota(jnp.int32, (t, t), 0)
    col = lax.broadcasted_iota(jnp.int32, (t, t), 1)
    q0 = pl.multiple_of(i * t, t)
    y = jnp.zeros((t, PAIR), F32)
    for hh in range(2):
        in_head = (lane >= hh * HEAD_DIM) & (lane < (hh + 1) * HEAD_DIM)
        qm = jnp.where(in_head, q, jnp.zeros_like(q))
        base = cum_ref[0, 0, hh:hh + 1, pl.ds(q0, LANES)][:, 0:1]

        def scores(kt, cum_k, mask):
            s = _dot(qm, kt) - (cum_k - base)
            if mask:
                s = jnp.where(col <= row, s, -MASKED)
            return s

        def update(m, acc, s, v):
            m_new = jnp.maximum(m, jnp.max(s, axis=-1, keepdims=True))
            p = jnp.exp(s - m_new).astype(BF16)
            pv = _dot(p, jnp.concatenate([v, ones[0:v.shape[0]]], axis=1))
            return m_new, jnp.exp(m - m_new) * acc + pv

        s = _dot(qm, ktm_ref[...]) - (cumm_ref[0, hh:hh + 1, :] - base)
        m = jnp.max(s, axis=-1, keepdims=True)
        p = jnp.exp(s - m).astype(BF16)
        acc = _dot(p, jnp.concatenate([vm_ref[...], ones[0:META_PAD]], axis=1))

        def body(c, carry):
            k0 = pl.multiple_of(c * t, t)
            s = scores(kt_ref[0, :, pl.ds(k0, t)], cum_ref[0, 0, hh:hh + 1, pl.ds(k0, t)], False)
            return update(*carry, s, v_ref[0, pl.ds(k0, t), :])

        m, acc = lax.fori_loop(0, i, body, (m, acc))
        s = scores(kt_ref[0, :, pl.ds(q0, t)], cum_ref[0, 0, hh:hh + 1, pl.ds(q0, t)], True)
        m, acc = update(m, acc, s, v_ref[0, pl.ds(q0, t), :])

        o = jnp.where(in_head, acc[:, 0:PAIR] / acc[:, PAIR:2 * PAIR], 0.0)
        ms = jnp.sum(o * o, axis=-1, keepdims=True) * (1.0 / HEAD_DIM)
        y = y + o * lax.rsqrt(ms + EPS)
    o_ref[0] = (y * g_ref[...]).astype(BF16)


def _attention(q, kt, v, cum, ktm, vm, cumm, gattn):
    nb, seq, _ = q.shape
    t = ATTN_TILE
    return pl.pallas_call(
        _attn_kernel,
        grid=(nb, N_PAIRS, seq // t),
        in_specs=[
            pl.BlockSpec((1, t, PAIR), lambda b, j, i: (b, i, j)),
            pl.BlockSpec((1, PAIR, seq), lambda b, j, i: (b, j, 0)),
            pl.BlockSpec((1, seq, PAIR), lambda b, j, i: (b, 0, j)),
            pl.BlockSpec((1, 1, 2, seq), lambda b, j, i: (b, j, 0, 0)),
            pl.BlockSpec((PAIR, META_PAD), lambda b, j, i: (j, 0)),
            pl.BlockSpec((META_PAD, PAIR), lambda b, j, i: (0, j)),
            pl.BlockSpec((1, 2, META_PAD), lambda b, j, i: (j, 0, 0)),
            pl.BlockSpec((1, PAIR), lambda b, j, i: (0, j)),
        ],
        out_specs=pl.BlockSpec((1, t, PAIR), lambda b, j, i: (b, i, j)),
        out_shape=jax.ShapeDtypeStruct((nb, seq, D_ATTN), BF16),
        compiler_params=pltpu.CompilerParams(
            dimension_semantics=("parallel", "parallel", "arbitrary"), vmem_limit_bytes=32 * 1024 * 1024),
        name="attn",
    )(q, kt, v, cum, ktm, vm, cumm, gattn)


def _mlp_kernel(x_ref, ya_ref, yc_ref, wo_ref, gmlp_ref, w1_ref, w2_ref, gfin_ref, o_ref):
    h = (x_ref[0] + _dot(ya_ref[0], wo_ref[0:D_ATTN, :]) + _dot(yc_ref[0], wo_ref[D_ATTN:D_ATTN + D_CONV, :]))
    hn = _rms_norm(h, gmlp_ref[...]).astype(BF16)
    ff = None
    for c in range(0, D_FF, FF_CHUNK):
        a = jnp.square(jnp.maximum(_dot(hn, w1_ref[:, c:c + FF_CHUNK]), 0.0)).astype(BF16)
        part = _dot(a, w2_ref[c:c + FF_CHUNK, :])
        ff = part if ff is None else ff + part
    o_ref[0] = _rms_norm(h + ff, gfin_ref[...])


def _mlp(x, ya, yc, wo, gmlp, w1, w2, gfin):
    nb, seq, _ = x.shape
    t = MLP_TILE
    const = lambda shape: pl.BlockSpec(shape, lambda b, i: (0,) * len(shape), pipeline_mode=pl.Buffered(1))
    return pl.pallas_call(
        _mlp_kernel,
        grid=(nb, seq // t),
        in_specs=[
            pl.BlockSpec((1, t, D_MODEL), lambda b, i: (b, i, 0)),
            pl.BlockSpec((1, t, D_ATTN), lambda b, i: (b, i, 0)),
            pl.BlockSpec((1, t, D_CONV), lambda b, i: (b, i, 0)),
            const(wo.shape), const(gmlp.shape), const(w1.shape), const(w2.shape), const(gfin.shape),
        ],
        out_specs=pl.BlockSpec((1, t, D_MODEL), lambda b, i: (b, i, 0)),
        out_shape=jax.ShapeDtypeStruct((nb, seq, D_MODEL), F32),
        compiler_params=pltpu.CompilerParams(
            dimension_semantics=("parallel", "parallel"), vmem_limit_bytes=56 * 1024 * 1024),
        name="mlp",
    )(x, ya, yc, wo, gmlp, w1, w2, gfin)


def kernel(x, meta_tokens, norm_mix_g, w_in, b_f, conv_w, out_norm_g, w_out, norm_mlp_g, w_ff1, w_ff2,
           final_norm_g):
    assert w_in.shape[0] == 1, "single-layer block"
    nb, seq, _ = x.shape
    w = w_in[0]
    o_q, o_k, o_v, o_f = 0, D_ATTN, 2 * D_ATTN, 3 * D_ATTN
    o_b = o_f + N_HEADS
    wq = w[:, o_q:o_k] * (HEAD_DIM ** -0.5)
    wtok = jnp.concatenate([wq, w[:, o_v:o_f], w[:, o_b:]], axis=1).astype(BF16)
    wkf = jnp.concatenate([w[:, o_k:o_v], w[:, o_f:o_b], jnp.zeros((D_MODEL, SUBLANES), F32)], axis=1)
    wkf = wkf.T.astype(BF16)
    gmix = norm_mix_g[0][None, :]
    bf_col = b_f[0][:, None]
    convw = jnp.concatenate([conv_w[0], jnp.zeros((SUBLANES - CONV_WIDTH, D_CONV), F32)], axis=0)
    gattn = out_norm_g[0, :D_ATTN][None, :]
    gconv = out_norm_g[0, D_ATTN:][None, :]
    group = jnp.arange(D_CONV) // HEAD_DIM
    gmat = ((group[:, None] == group[None, :]).astype(F32) / HEAD_DIM).astype(BF16)

    meta = jnp.zeros((1, META_PAD, D_MODEL), F32).at[0, :N_META].set(meta_tokens)
    proj = functools.partial(_projection, gmix=gmix, wtok=wtok, wkf=wkf, bf_col=bf_col, convw=convw,
                             gconv=gconv, gmat=gmat)
    ktm, vm, cumm, tailm = proj(meta, cum0=jnp.zeros((N_HEADS, LANES), F32),
                                tail0=jnp.zeros((SUBLANES, D_CONV), F32), tile=META_PAD, meta=True)
    q, kt, v, cum, yc = proj(x, cum0=cumm[0], tail0=tailm, tile=PROJ_TILE, meta=False)

    ya = _attention(q, kt, v, cum.reshape(nb, N_PAIRS, 2, seq), ktm[0], vm[0],
                    cumm[0].reshape(N_PAIRS, 2, META_PAD), gattn)

    return _mlp(x, ya, yc, w_out[0].astype(BF16), norm_mlp_g[0][None, :], w_ff1[0].astype(BF16),
                w_ff2[0].astype(BF16), final_norm_g[None, :])
```

```python
import functools

import jax
import jax.numpy as jnp
from jax import lax
from jax.experimental import pallas as pl
from jax.experimental.pallas import tpu as pltpu

D_MODEL = 1024
N_META = 16
HEAD_DIM = 64
D_ATTN = 512
N_HEADS = D_ATTN // HEAD_DIM
D_CONV = 512
CONV_WIDTH = 3
D_FF = 4096
EPS = 1e-5
MASKED = -1e30
LOG2E = 1.4426950408889634
BIAS_TERMS = 3

LANES = 128
SUBLANES = 8
BF16_ROWS = 16
PAIR = 2 * HEAD_DIM
N_PAIRS = N_HEADS // 2
META_PAD = LANES

PROJ_TILE = 512
ATTN_TQ = 256
ATTN_TK = 256
MLP_TILE = 512
FF_CHUNK = 1024

BF16 = jnp.bfloat16
F32 = jnp.float32


def _dot(a, b):
    return jnp.dot(a, b, preferred_element_type=F32)


def _dot_nt(a, b):
    return lax.dot_general(a, b, (((1,), (1,)), ((), ())), preferred_element_type=F32)


def _rms_norm(x, g):
    ms = jnp.mean(x * x, axis=-1, keepdims=True)
    return x * lax.rsqrt(ms + EPS) * g


def _split_bf16(x, parts):
    out = []
    r = x
    for _ in range(parts):
        t = r.astype(BF16)
        out.append(t)
        r = r - t.astype(F32)
    return out


def _stack_terms(x):
    terms = [t.astype(F32) for t in _split_bf16(x, BIAS_TERMS)]
    return jnp.concatenate(terms + [jnp.zeros_like(x)], axis=0).astype(BF16)


def _split_dot(x, w):
    sums = _dot(_stack_terms(x), w)
    h = x.shape[0]
    return sums[0:h] + sums[h:2 * h] + sums[2 * h:3 * h]


def _proj_kernel(x_ref, gmix_ref, wtok_ref, wrow_ref, bf_ref, convw_ref, gconv_ref, gmat_ref, tri_ref, perm_ref,
                 cum0_ref, tail0_ref, *refs, tile, meta):
    if meta:
        ka_ref, vt_ref, cum_ref, tail_ref, cu_scr, carry_scr = refs
    else:
        qt_ref, ka_ref, vt_ref, yc_ref, cu_scr, carry_scr = refs

    @pl.when(pl.program_id(1) == 0)
    def _():
        carry_scr[...] = jnp.broadcast_to(cum0_ref[:, N_META - 1:N_META], carry_scr.shape)
        cu_scr[0:SUBLANES, :] = tail0_ref[...]

    xn = _rms_norm(x_ref[0], gmix_ref[...]).astype(BF16)

    rows = _dot_nt(wrow_ref[...], xn)
    if not meta:
        qt_ref[0] = rows[0:D_ATTN].astype(BF16)
    vt_ref[0] = rows[D_ATTN:2 * D_ATTN].astype(BF16)
    z = rows[2 * D_ATTN:2 * D_ATTN + N_HEADS] + bf_ref[...]
    log_f = jnp.minimum(z, 0.0) - jnp.log1p(jnp.exp(-jnp.abs(z)))
    cum = _split_dot(log_f, tri_ref[...]) + carry_scr[:, 0:1]
    carry_scr[...] = jnp.broadcast_to(cum[:, tile - 1:tile], carry_scr.shape)
    bias = -LOG2E * cum
    if meta:
        cum_ref[...] = cum
        lane = lax.broadcasted_iota(jnp.int32, cum.shape, 1)
        bias = jnp.where(lane < N_META, bias, MASKED)
    bias_cols = _dot(perm_ref[...], _stack_terms(bias)).T

    k = _dot(xn, wtok_ref[:, 0:D_ATTN])
    low = lax.broadcasted_iota(jnp.int32, (1, PAIR), 1) < HEAD_DIM
    for h in range(N_HEADS):
        kp = k[:, (h // 2) * PAIR:(h // 2 + 1) * PAIR]
        ka_ref[0, h] = jnp.where(low == (h % 2 == 0), kp, bias_cols).astype(BF16)

    c0 = D_ATTN
    cu = _dot(xn, wtok_ref[:, c0 + D_CONV:c0 + 2 * D_CONV]) * _dot(xn, wtok_ref[:, c0 + 2 * D_CONV:c0 + 3 * D_CONV])
    cu_scr[SUBLANES:SUBLANES + tile, :] = cu
    if meta:
        tail_ref[...] = cu_scr[N_META:N_META + SUBLANES, :]
        return
    cu1 = cu_scr[SUBLANES - 1:SUBLANES - 1 + tile, :]
    cu2 = cu_scr[SUBLANES - 2:SUBLANES - 2 + tile, :]
    conv = convw_ref[0:1, :] * cu2 + convw_ref[1:2, :] * cu1 + convw_ref[2:3, :] * cu
    cu_scr[0:SUBLANES, :] = cu[tile - SUBLANES:tile, :]
    yc = _dot(xn, wtok_ref[:, c0:c0 + D_CONV]) * conv
    sq_hi, sq_lo = _split_bf16(yc * yc, 2)
    ms = _dot(sq_hi, gmat_ref[...]) + _dot(sq_lo, gmat_ref[...])
    yc_ref[0] = (yc * lax.rsqrt(ms + EPS) * gconv_ref[...]).astype(BF16)


def _projection(x, gmix, wtok, wrow, bf_col, convw, gconv, gmat, cum0, tail0, *, tile, meta):
    nb, seq, _ = x.shape
    nt = seq // tile
    tri = jnp.triu(jnp.ones((tile, tile), F32)).astype(BF16)
    out_lane = jnp.arange(PAIR) % HEAD_DIM
    src_row = jnp.where(out_lane < BIAS_TERMS * N_HEADS, out_lane, -1)
    perm = (src_row[:, None] == jnp.arange(4 * N_HEADS)[None, :]).astype(BF16)
    const = lambda shape: pl.BlockSpec(shape, lambda b, t: (0,) * len(shape))
    in_specs = [
        pl.BlockSpec((1, tile, D_MODEL), lambda b, t: (b, t, 0)),
        const(gmix.shape), const(wtok.shape), const(wrow.shape), const(bf_col.shape), const(convw.shape),
        const(gconv.shape), const(gmat.shape), const(tri.shape), const(perm.shape), const(cum0.shape),
        const(tail0.shape),
    ]
    tok_spec = pl.BlockSpec((1, tile, D_CONV), lambda b, t: (b, t, 0))
    row_spec = pl.BlockSpec((1, D_ATTN, tile), lambda b, t: (b, 0, t))
    ka_spec = pl.BlockSpec((1, N_HEADS, tile, PAIR), lambda b, t: (b, 0, t, 0))
    tok_shape = jax.ShapeDtypeStruct((nb, seq, D_CONV), BF16)
    row_shape = jax.ShapeDtypeStruct((nb, D_ATTN, seq), BF16)
    ka_shape = jax.ShapeDtypeStruct((nb, N_HEADS, seq, PAIR), BF16)
    if meta:
        out_specs = [ka_spec, row_spec, const((N_HEADS, tile)), const((SUBLANES, D_CONV))]
        out_shape = [ka_shape, row_shape, jax.ShapeDtypeStruct((N_HEADS, tile), F32),
                     jax.ShapeDtypeStruct((SUBLANES, D_CONV), F32)]
    else:
        out_specs = [row_spec, ka_spec, row_spec, tok_spec]
        out_shape = [row_shape, ka_shape, row_shape, tok_shape]
    return pl.pallas_call(
        functools.partial(_proj_kernel, tile=tile, meta=meta),
        grid=(nb, nt),
        in_specs=in_specs,
        out_specs=out_specs,
        out_shape=out_shape,
        scratch_shapes=[pltpu.VMEM((tile + SUBLANES, D_CONV), F32), pltpu.VMEM((N_HEADS, LANES), F32)],
        compiler_params=pltpu.CompilerParams(
            dimension_semantics=("arbitrary", "arbitrary"), vmem_limit_bytes=48 * 1024 * 1024),
        name="proj_meta" if meta else "proj",
    )(x, gmix, wtok, wrow, bf_col, convw, gconv, gmat, tri, perm, cum0, tail0)


def _attn_kernel(qt_ref, ka_ref, vt_ref, kam_ref, vtm_ref, g_ref, o_ref, m_scr, a_scr, acc_scr, s_scr, p_scr):
    i = pl.program_id(1)
    tq, tk = ATTN_TQ, ATTN_TK
    sub = lax.broadcasted_iota(jnp.int32, (HEAD_DIM, tq), 0)
    ones = jnp.ones((BF16_ROWS, max(tk, META_PAD)), BF16)

    def query(h):
        qh = qt_ref[0, h * HEAD_DIM:(h + 1) * HEAD_DIM, :]
        ind = ((sub % N_HEADS == h) & (sub < BIAS_TERMS * N_HEADS)).astype(BF16)
        return jnp.concatenate([qh, ind] if h % 2 == 0 else [ind, qh], axis=0)

    def values(vt, h, k0, width):
        return jnp.concatenate([vt[h * HEAD_DIM:(h + 1) * HEAD_DIM, pl.ds(k0, width)], ones[:, 0:width]], axis=0)

    qs = [query(h) for h in range(N_HEADS)]

    def step(keys, vt, k0, width, diag_offset=None, first=False):
        for h in range(N_HEADS):
            s = _dot(keys(h), qs[h])
            if diag_offset is not None:
                key = lax.broadcasted_iota(jnp.int32, (width, tq), 0) + diag_offset
                s = jnp.where(key <= lax.broadcasted_iota(jnp.int32, (width, tq), 1), s, MASKED)
            s_scr[h, 0:width] = s
        for h in range(N_HEADS):
            m_new = jnp.max(s_scr[h, 0:width], axis=0, keepdims=True)
            if not first:
                m = m_scr[h]
                m_new = jnp.maximum(m, m_new)
                a_scr[h] = jnp.exp2(m - m_new)
            m_scr[h] = m_new
            p_scr[h, 0:width] = jnp.exp2(s_scr[h, 0:width] - m_new).astype(BF16)
        for h in range(N_HEADS):
            pv = _dot(values(vt, h, k0, width), p_scr[h, 0:width])
            acc_scr[h] = pv if first else a_scr[h] * acc_scr[h] + pv

    def real_step(k0, diag_offset=None):
        step(lambda h: ka_ref[0, h, pl.ds(k0, tk), :], vt_ref.at[0], k0, tk, diag_offset)

    def body(c, carry):
        real_step(pl.multiple_of(c * tk, tk))
        return carry

    step(lambda h: kam_ref[h], vtm_ref, 0, META_PAD, first=True)
    lax.fori_loop(0, i * (tq // tk), body, 0)
    for d in range(tq // tk):
        real_step(pl.multiple_of(i * tq + d * tk, tk), d * tk)

    def normed(h):
        acc = acc_scr[h]
        o = acc[0:HEAD_DIM] / acc[HEAD_DIM:HEAD_DIM + 1]
        return o * lax.rsqrt(jnp.mean(o * o, axis=0, keepdims=True) + EPS)

    for j in range(N_PAIRS):
        y = jnp.concatenate([normed(2 * j), normed(2 * j + 1)], axis=0).T
        o_ref[0, :, j * PAIR:(j + 1) * PAIR] = (y * g_ref[:, j * PAIR:(j + 1) * PAIR]).astype(BF16)


def _attention(qt, ka, vt, kam, vtm, gattn):
    nb, _, seq = qt.shape
    tq = ATTN_TQ
    whole = lambda a: pl.BlockSpec(a.shape, lambda b, i: (0,) * a.ndim)
    per_batch = lambda a: pl.BlockSpec((1,) + a.shape[1:], lambda b, i: (b,) + (0,) * (a.ndim - 1))
    return pl.pallas_call(
        _attn_kernel,
        grid=(nb, seq // tq),
        in_specs=[
            pl.BlockSpec((1, D_ATTN, tq), lambda b, i: (b, 0, i)),
            per_batch(ka), per_batch(vt), whole(kam), whole(vtm), whole(gattn),
        ],
        out_specs=pl.BlockSpec((1, tq, D_ATTN), lambda b, i: (b, i, 0)),
        out_shape=jax.ShapeDtypeStruct((nb, seq, D_ATTN), BF16),
        scratch_shapes=[
            pltpu.VMEM((N_HEADS, 1, tq), F32), pltpu.VMEM((N_HEADS, 1, tq), F32),
            pltpu.VMEM((N_HEADS, HEAD_DIM + BF16_ROWS, tq), F32),
            pltpu.VMEM((N_HEADS, ATTN_TK, tq), F32), pltpu.VMEM((N_HEADS, ATTN_TK, tq), BF16),
        ],
        compiler_params=pltpu.CompilerParams(
            dimension_semantics=("arbitrary", "arbitrary"), vmem_limit_bytes=48 * 1024 * 1024),
        name="attn",
    )(qt, ka, vt, kam, vtm, gattn)


def _mlp_kernel(x_ref, ya_ref, yc_ref, wo_ref, gmlp_ref, w1_ref, w2_ref, gfin_ref, o_ref):
    h = (x_ref[0] + _dot(ya_ref[0], wo_ref[0:D_ATTN, :]) + _dot(yc_ref[0], wo_ref[D_ATTN:D_ATTN + D_CONV, :]))
    hn = _rms_norm(h, gmlp_ref[...]).astype(BF16)
    ff = None
    for c in range(0, D_FF, FF_CHUNK):
        a = jnp.square(jnp.maximum(_dot(hn, w1_ref[:, c:c + FF_CHUNK]), 0.0)).astype(BF16)
        part = _dot(a, w2_ref[c:c + FF_CHUNK, :])
        ff = part if ff is None else ff + part
    o_ref[0] = _rms_norm(h + ff, gfin_ref[...])


def _mlp(x, ya, yc, wo, gmlp, w1, w2, gfin):
    nb, seq, _ = x.shape
    t = MLP_TILE
    const = lambda shape: pl.BlockSpec(shape, lambda b, i: (0,) * len(shape), pipeline_mode=pl.Buffered(1))
    return pl.pallas_call(
        _mlp_kernel,
        grid=(nb, seq // t),
        in_specs=[
            pl.BlockSpec((1, t, D_MODEL), lambda b, i: (b, i, 0)),
            pl.BlockSpec((1, t, D_ATTN), lambda b, i: (b, i, 0)),
            pl.BlockSpec((1, t, D_CONV), lambda b, i: (b, i, 0)),
            const(wo.shape), const(gmlp.shape), const(w1.shape), const(w2.shape), const(gfin.shape),
        ],
        out_specs=pl.BlockSpec((1, t, D_MODEL), lambda b, i: (b, i, 0)),
        out_shape=jax.ShapeDtypeStruct((nb, seq, D_MODEL), F32),
        compiler_params=pltpu.CompilerParams(
            dimension_semantics=("parallel", "parallel"), vmem_limit_bytes=56 * 1024 * 1024),
        name="mlp",
    )(x, ya, yc, wo, gmlp, w1, w2, gfin)


def kernel(x, meta_tokens, norm_mix_g, w_in, b_f, conv_w, out_norm_g, w_out, norm_mlp_g, w_ff1, w_ff2,
           final_norm_g):
    assert w_in.shape[0] == 1, "single-layer block"
    w = w_in[0]
    o_q, o_k, o_v, o_f = 0, D_ATTN, 2 * D_ATTN, 3 * D_ATTN
    o_b = o_f + N_HEADS
    wq = w[:, o_q:o_k] * (LOG2E * HEAD_DIM ** -0.5)
    wtok = jnp.concatenate([w[:, o_k:o_v], w[:, o_b:]], axis=1).astype(BF16)
    wrow = jnp.concatenate([wq, w[:, o_v:o_f], w[:, o_f:o_b], jnp.zeros((D_MODEL, SUBLANES), F32)], axis=1)
    wrow = wrow.T.astype(BF16)
    gmix = norm_mix_g[0][None, :]
    bf_col = b_f[0][:, None]
    convw = jnp.concatenate([conv_w[0], jnp.zeros((SUBLANES - CONV_WIDTH, D_CONV), F32)], axis=0)
    gattn = out_norm_g[0, :D_ATTN][None, :]
    gconv = out_norm_g[0, D_ATTN:][None, :]
    group = jnp.arange(D_CONV) // HEAD_DIM
    gmat = ((group[:, None] == group[None, :]).astype(F32) / HEAD_DIM).astype(BF16)

    meta = jnp.zeros((1, META_PAD, D_MODEL), F32).at[0, :N_META].set(meta_tokens)
    proj = functools.partial(_projection, gmix=gmix, wtok=wtok, wrow=wrow, bf_col=bf_col, convw=convw,
                             gconv=gconv, gmat=gmat)
    kam, vtm, cumm, tailm = proj(meta, cum0=jnp.zeros((N_HEADS, LANES), F32),
                                 tail0=jnp.zeros((SUBLANES, D_CONV), F32), tile=META_PAD, meta=True)
    qt, ka, vt, yc = proj(x, cum0=cumm, tail0=tailm, tile=PROJ_TILE, meta=False)

    ya = _attention(qt, ka, vt, kam[0], vtm[0], gattn)

    return _mlp(x, ya, yc, w_out[0].astype(BF16), norm_mlp_g[0][None, :], w_ff1[0].astype(BF16),
                w_ff2[0].astype(BF16), final_norm_g[None, :])
```

```python
import functools

import jax
import jax.numpy as jnp
from jax import lax
from jax.experimental import pallas as pl
from jax.experimental.pallas import tpu as pltpu

D_MODEL = 1024
N_META = 16
HEAD_DIM = 64
D_ATTN = 512
N_HEADS = D_ATTN // HEAD_DIM
D_CONV = 512
CONV_WIDTH = 3
D_FF = 4096
EPS = 1e-5
MASKED = -1e30
LOG2E = 1.4426950408889634
BIAS_TERMS = 3

LANES = 128
SUBLANES = 8
BF16_ROWS = 16
PAIR = 2 * HEAD_DIM
N_PAIRS = N_HEADS // 2
META_PAD = LANES

PROJ_TILE = 512
ATTN_TQ = 512
ATTN_TK = 256
MLP_TILE = 512
FF_CHUNK = 1024

BF16 = jnp.bfloat16
F32 = jnp.float32


def _dot(a, b):
    return jnp.dot(a, b, preferred_element_type=F32)


def _dot_nt(a, b):
    return lax.dot_general(a, b, (((1,), (1,)), ((), ())), preferred_element_type=F32)


def _rms_norm(x, g):
    ms = jnp.mean(x * x, axis=-1, keepdims=True)
    return x * lax.rsqrt(ms + EPS) * g


def _split_bf16(x, parts):
    out = []
    r = x
    for _ in range(parts):
        t = r.astype(BF16)
        out.append(t)
        r = r - t.astype(F32)
    return out


def _stack_terms(x):
    terms = [t.astype(F32) for t in _split_bf16(x, BIAS_TERMS)]
    return jnp.concatenate(terms + [jnp.zeros_like(x)], axis=0).astype(BF16)


def _split_dot(x, w):
    sums = _dot(_stack_terms(x), w)
    h = x.shape[0]
    return sums[0:h] + sums[h:2 * h] + sums[2 * h:3 * h]


def _proj_kernel(x_ref, gmix_ref, wtok_ref, wrow_ref, bf_ref, convw_ref, gconv_ref, gmat_ref, tri_ref, perm_ref,
                 cum0_ref, tail0_ref, *refs, tile, meta):
    if meta:
        ka_ref, vt_ref, cum_ref, tail_ref, cu_scr, carry_scr = refs
    else:
        qt_ref, ka_ref, vt_ref, yc_ref, cu_scr, carry_scr = refs

    @pl.when(pl.program_id(1) == 0)
    def _():
        carry_scr[...] = jnp.broadcast_to(cum0_ref[:, N_META - 1:N_META], carry_scr.shape)
        cu_scr[0:SUBLANES, :] = tail0_ref[...]

    xn = _rms_norm(x_ref[0], gmix_ref[...]).astype(BF16)

    rows = _dot_nt(wrow_ref[...], xn)
    if not meta:
        qt_ref[0] = rows[0:D_ATTN].astype(BF16)
    vt_ref[0] = rows[D_ATTN:2 * D_ATTN].astype(BF16)
    z = rows[2 * D_ATTN:2 * D_ATTN + N_HEADS] + bf_ref[...]
    log_f = jnp.minimum(z, 0.0) - jnp.log1p(jnp.exp(-jnp.abs(z)))
    cum = _split_dot(log_f, tri_ref[...]) + carry_scr[:, 0:1]
    carry_scr[...] = jnp.broadcast_to(cum[:, tile - 1:tile], carry_scr.shape)
    bias = -LOG2E * cum
    if meta:
        cum_ref[...] = cum
        lane = lax.broadcasted_iota(jnp.int32, cum.shape, 1)
        bias = jnp.where(lane < N_META, bias, MASKED)
    bias_cols = _dot(perm_ref[...], _stack_terms(bias)).T

    k = _dot(xn, wtok_ref[:, 0:D_ATTN])
    low = lax.broadcasted_iota(jnp.int32, (1, PAIR), 1) < HEAD_DIM
    for h in range(N_HEADS):
        kp = k[:, (h // 2) * PAIR:(h // 2 + 1) * PAIR]
        ka_ref[0, h] = jnp.where(low == (h % 2 == 0), kp, bias_cols).astype(BF16)

    c0 = D_ATTN
    cu = _dot(xn, wtok_ref[:, c0 + D_CONV:c0 + 2 * D_CONV]) * _dot(xn, wtok_ref[:, c0 + 2 * D_CONV:c0 + 3 * D_CONV])
    cu_scr[SUBLANES:SUBLANES + tile, :] = cu
    if meta:
        tail_ref[...] = cu_scr[N_META:N_META + SUBLANES, :]
        return
    cu1 = cu_scr[SUBLANES - 1:SUBLANES - 1 + tile, :]
    cu2 = cu_scr[SUBLANES - 2:SUBLANES - 2 + tile, :]
    conv = convw_ref[0:1, :] * cu2 + convw_ref[1:2, :] * cu1 + convw_ref[2:3, :] * cu
    cu_scr[0:SUBLANES, :] = cu[tile - SUBLANES:tile, :]
    yc = _dot(xn, wtok_ref[:, c0:c0 + D_CONV]) * conv
    ms = _dot((yc * yc).astype(BF16), gmat_ref[...])
    yc_ref[0] = (yc * lax.rsqrt(ms + EPS) * gconv_ref[...]).astype(BF16)


def _projection(x, gmix, wtok, wrow, bf_col, convw, gconv, gmat, cum0, tail0, *, tile, meta):
    nb, seq, _ = x.shape
    nt = seq // tile
    tri = jnp.triu(jnp.ones((tile, tile), F32)).astype(BF16)
    out_lane = jnp.arange(PAIR) % HEAD_DIM
    src_row = jnp.where(out_lane < BIAS_TERMS * N_HEADS, out_lane, -1)
    perm = (src_row[:, None] == jnp.arange(4 * N_HEADS)[None, :]).astype(BF16)
    const = lambda shape: pl.BlockSpec(shape, lambda b, t: (0,) * len(shape))
    in_specs = [
        pl.BlockSpec((1, tile, D_MODEL), lambda b, t: (b, t, 0)),
        const(gmix.shape), const(wtok.shape), const(wrow.shape), const(bf_col.shape), const(convw.shape),
        const(gconv.shape), const(gmat.shape), const(tri.shape), const(perm.shape), const(cum0.shape),
        const(tail0.shape),
    ]
    tok_spec = pl.BlockSpec((1, tile, D_CONV), lambda b, t: (b, t, 0))
    row_spec = pl.BlockSpec((1, D_ATTN, tile), lambda b, t: (b, 0, t))
    ka_spec = pl.BlockSpec((1, N_HEADS, tile, PAIR), lambda b, t: (b, 0, t, 0))
    tok_shape = jax.ShapeDtypeStruct((nb, seq, D_CONV), BF16)
    row_shape = jax.ShapeDtypeStruct((nb, D_ATTN, seq), BF16)
    ka_shape = jax.ShapeDtypeStruct((nb, N_HEADS, seq, PAIR), BF16)
    if meta:
        out_specs = [ka_spec, row_spec, const((N_HEADS, tile)), const((SUBLANES, D_CONV))]
        out_shape = [ka_shape, row_shape, jax.ShapeDtypeStruct((N_HEADS, tile), F32),
                     jax.ShapeDtypeStruct((SUBLANES, D_CONV), F32)]
    else:
        out_specs = [row_spec, ka_spec, row_spec, tok_spec]
        out_shape = [row_shape, ka_shape, row_shape, tok_shape]
    return pl.pallas_call(
        functools.partial(_proj_kernel, tile=tile, meta=meta),
        grid=(nb, nt),
        in_specs=in_specs,
        out_specs=out_specs,
        out_shape=out_shape,
        scratch_shapes=[pltpu.VMEM((tile + SUBLANES, D_CONV), F32), pltpu.VMEM((N_HEADS, LANES), F32)],
        compiler_params=pltpu.CompilerParams(
            dimension_semantics=("arbitrary", "arbitrary"), vmem_limit_bytes=48 * 1024 * 1024),
        name="proj_meta" if meta else "proj",
    )(x, gmix, wtok, wrow, bf_col, convw, gconv, gmat, tri, perm, cum0, tail0)


def _attn_kernel(qt_ref, ka_ref, vt_ref, kam_ref, vtm_ref, g_ref, o_ref, m_scr, a_scr, acc_scr, s_scr, p_scr):
    i = pl.program_id(1)
    tq, tk = ATTN_TQ, ATTN_TK
    sub = lax.broadcasted_iota(jnp.int32, (HEAD_DIM, tq), 0)
    ones = jnp.ones((BF16_ROWS, max(tk, META_PAD)), BF16)

    def query(h):
        qh = qt_ref[0, h * HEAD_DIM:(h + 1) * HEAD_DIM, :]
        ind = ((sub % N_HEADS == h) & (sub < BIAS_TERMS * N_HEADS)).astype(BF16)
        return jnp.concatenate([qh, ind] if h % 2 == 0 else [ind, qh], axis=0)

    def values(vt, h, k0, width):
        return jnp.concatenate([vt[h * HEAD_DIM:(h + 1) * HEAD_DIM, pl.ds(k0, width)], ones[:, 0:width]], axis=0)

    qs = [query(h) for h in range(N_HEADS)]

    def step(keys, vt, k0, width, diag_offset=None, first=False):
        for h in range(N_HEADS):
            s = _dot(keys(h), qs[h])
            if diag_offset is not None:
                key = lax.broadcasted_iota(jnp.int32, (width, tq), 0) + diag_offset
                s = jnp.where(key <= lax.broadcasted_iota(jnp.int32, (width, tq), 1), s, MASKED)
            s_scr[h, 0:width] = s
        for h in range(N_HEADS):
            m_new = jnp.max(s_scr[h, 0:width], axis=0, keepdims=True)
            if not first:
                m = m_scr[h]
                m_new = jnp.maximum(m, m_new)
                a_scr[h] = jnp.exp2(m - m_new)
            m_scr[h] = m_new
            p_scr[h, 0:width] = jnp.exp2(s_scr[h, 0:width] - m_new).astype(BF16)
        for h in range(N_HEADS):
            pv = _dot(values(vt, h, k0, width), p_scr[h, 0:width])
            acc_scr[h] = pv if first else a_scr[h] * acc_scr[h] + pv

    def real_step(k0, diag_offset=None):
        step(lambda h: ka_ref[0, h, pl.ds(k0, tk), :], vt_ref.at[0], k0, tk, diag_offset)

    def body(c, carry):
        real_step(pl.multiple_of(c * tk, tk))
        return carry

    step(lambda h: kam_ref[h], vtm_ref, 0, META_PAD, first=True)
    lax.fori_loop(0, i * (tq // tk), body, 0)
    for d in range(tq // tk):
        real_step(pl.multiple_of(i * tq + d * tk, tk), d * tk)

    def normed(h):
        acc = acc_scr[h]
        o = acc[0:HEAD_DIM] / acc[HEAD_DIM:HEAD_DIM + 1]
        return o * lax.rsqrt(jnp.mean(o * o, axis=0, keepdims=True) + EPS)

    for j in range(N_PAIRS):
        y = jnp.concatenate([normed(2 * j), normed(2 * j + 1)], axis=0).T
        o_ref[0, :, j * PAIR:(j + 1) * PAIR] = (y * g_ref[:, j * PAIR:(j + 1) * PAIR]).astype(BF16)


def _attention(qt, ka, vt, kam, vtm, gattn):
    nb, _, seq = qt.shape
    tq = ATTN_TQ
    whole = lambda a: pl.BlockSpec(a.shape, lambda b, i: (0,) * a.ndim)
    per_batch = lambda a: pl.BlockSpec((1,) + a.shape[1:], lambda b, i: (b,) + (0,) * (a.ndim - 1))
    return pl.pallas_call(
        _attn_kernel,
        grid=(nb, seq // tq),
        in_specs=[
            pl.BlockSpec((1, D_ATTN, tq), lambda b, i: (b, 0, i)),
            per_batch(ka), per_batch(vt), whole(kam), whole(vtm), whole(gattn),
        ],
        out_specs=pl.BlockSpec((1, tq, D_ATTN), lambda b, i: (b, i, 0)),
        out_shape=jax.ShapeDtypeStruct((nb, seq, D_ATTN), BF16),
        scratch_shapes=[
            pltpu.VMEM((N_HEADS, 1, tq), F32), pltpu.VMEM((N_HEADS, 1, tq), F32),
            pltpu.VMEM((N_HEADS, HEAD_DIM + BF16_ROWS, tq), F32),
            pltpu.VMEM((N_HEADS, ATTN_TK, tq), F32), pltpu.VMEM((N_HEADS, ATTN_TK, tq), BF16),
        ],
        compiler_params=pltpu.CompilerParams(
            dimension_semantics=("arbitrary", "arbitrary"), vmem_limit_bytes=48 * 1024 * 1024),
        name="attn",
    )(qt, ka, vt, kam, vtm, gattn)


def _mlp_kernel(x_ref, ya_ref, yc_ref, wo_ref, gmlp_ref, w1_ref, w2_ref, gfin_ref, o_ref):
    h = (x_ref[0] + _dot(ya_ref[0], wo_ref[0:D_ATTN, :]) + _dot(yc_ref[0], wo_ref[D_ATTN:D_ATTN + D_CONV, :]))
    hn = _rms_norm(h, gmlp_ref[...]).astype(BF16)
    ff = None
    for c in range(0, D_FF, FF_CHUNK):
        a = jnp.square(jnp.maximum(_dot(hn, w1_ref[:, c:c + FF_CHUNK]), 0.0)).astype(BF16)
        part = _dot(a, w2_ref[c:c + FF_CHUNK, :])
        ff = part if ff is None else ff + part
    o_ref[0] = _rms_norm(h + ff, gfin_ref[...])


def _mlp(x, ya, yc, wo, gmlp, w1, w2, gfin):
    nb, seq, _ = x.shape
    t = MLP_TILE
    const = lambda shape: pl.BlockSpec(shape, lambda b, i: (0,) * len(shape), pipeline_mode=pl.Buffered(1))
    return pl.pallas_call(
        _mlp_kernel,
        grid=(nb, seq // t),
        in_specs=[
            pl.BlockSpec((1, t, D_MODEL), lambda b, i: (b, i, 0)),
            pl.BlockSpec((1, t, D_ATTN), lambda b, i: (b, i, 0)),
            pl.BlockSpec((1, t, D_CONV), lambda b, i: (b, i, 0)),
            const(wo.shape), const(gmlp.shape), const(w1.shape), const(w2.shape), const(gfin.shape),
        ],
        out_specs=pl.BlockSpec((1, t, D_MODEL), lambda b, i: (b, i, 0)),
        out_shape=jax.ShapeDtypeStruct((nb, seq, D_MODEL), F32),
        compiler_params=pltpu.CompilerParams(
            dimension_semantics=("parallel", "parallel"), vmem_limit_bytes=56 * 1024 * 1024),
        name="mlp",
    )(x, ya, yc, wo, gmlp, w1, w2, gfin)


def kernel(x, meta_tokens, norm_mix_g, w_in, b_f, conv_w, out_norm_g, w_out, norm_mlp_g, w_ff1, w_ff2,
           final_norm_g):
    assert w_in.shape[0] == 1, "single-layer block"
    w = w_in[0]
    o_q, o_k, o_v, o_f = 0, D_ATTN, 2 * D_ATTN, 3 * D_ATTN
    o_b = o_f + N_HEADS
    wq = w[:, o_q:o_k] * (LOG2E * HEAD_DIM ** -0.5)
    wtok = jnp.concatenate([w[:, o_k:o_v], w[:, o_b:]], axis=1).astype(BF16)
    wrow = jnp.concatenate([wq, w[:, o_v:o_f], w[:, o_f:o_b], jnp.zeros((D_MODEL, SUBLANES), F32)], axis=1)
    wrow = wrow.T.astype(BF16)
    gmix = norm_mix_g[0][None, :]
    bf_col = b_f[0][:, None]
    convw = jnp.concatenate([conv_w[0], jnp.zeros((SUBLANES - CONV_WIDTH, D_CONV), F32)], axis=0)
    gattn = out_norm_g[0, :D_ATTN][None, :]
    gconv = out_norm_g[0, D_ATTN:][None, :]
    group = jnp.arange(D_CONV) // HEAD_DIM
    gmat = ((group[:, None] == group[None, :]).astype(F32) / HEAD_DIM).astype(BF16)

    meta = jnp.zeros((1, META_PAD, D_MODEL), F32).at[0, :N_META].set(meta_tokens)
    proj = functools.partial(_projection, gmix=gmix, wtok=wtok, wrow=wrow, bf_col=bf_col, convw=convw,
                             gconv=gconv, gmat=gmat)
    kam, vtm, cumm, tailm = proj(meta, cum0=jnp.zeros((N_HEADS, LANES), F32),
                                 tail0=jnp.zeros((SUBLANES, D_CONV), F32), tile=META_PAD, meta=True)
    qt, ka, vt, yc = proj(x, cum0=cumm, tail0=tailm, tile=PROJ_TILE, meta=False)

    ya = _attention(qt, ka, vt, kam[0], vtm[0], gattn)

    return _mlp(x, ya, yc, w_out[0].astype(BF16), norm_mlp_g[0][None, :], w_ff1[0].astype(BF16),
                w_ff2[0].astype(BF16), final_norm_g[None, :])
```

```python
import functools

import jax
import jax.numpy as jnp
from jax import lax
from jax.experimental import pallas as pl
from jax.experimental.pallas import tpu as pltpu

D_MODEL = 1024
N_META = 16
HEAD_DIM = 64
D_ATTN = 512
N_HEADS = D_ATTN // HEAD_DIM
D_CONV = 512
CONV_WIDTH = 3
D_FF = 4096
EPS = 1e-5
MASKED = -1e30
LOG2E = 1.4426950408889634
BIAS_TERMS = 3

LANES = 128
SUBLANES = 8
BF16_ROWS = 16
PAIR = 2 * HEAD_DIM
N_PAIRS = N_HEADS // 2
META_PAD = LANES

PROJ_TILE = 512
ATTN_TQ = 512
ATTN_TK = 256
MLP_TILE = 512
FF_CHUNK = 1024

BF16 = jnp.bfloat16
F32 = jnp.float32


def _dot(a, b):
    return jnp.dot(a, b, preferred_element_type=F32)


def _dot_nt(a, b):
    return lax.dot_general(a, b, (((1,), (1,)), ((), ())), preferred_element_type=F32)


def _rms_norm(x, g):
    ms = jnp.mean(x * x, axis=-1, keepdims=True)
    return x * lax.rsqrt(ms + EPS) * g


def _split_bf16(x, parts):
    out = []
    r = x
    for _ in range(parts):
        t = r.astype(BF16)
        out.append(t)
        r = r - t.astype(F32)
    return out


def _stack_terms(x):
    terms = [t.astype(F32) for t in _split_bf16(x, BIAS_TERMS)]
    return jnp.concatenate(terms + [jnp.zeros_like(x)], axis=0).astype(BF16)


def _split_dot(x, w):
    sums = _dot(_stack_terms(x), w)
    h = x.shape[0]
    return sums[0:h] + sums[h:2 * h] + sums[2 * h:3 * h]


def _proj_kernel(x_ref, gmix_ref, wtok_ref, wrow_ref, bf_ref, convw_ref, gconv_ref, gmat_ref, tri_ref, perm_ref,
                 cum0_ref, tail0_ref, *refs, tile, meta):
    if meta:
        ka_ref, vt_ref, cum_ref, tail_ref, cu_scr, carry_scr = refs
    else:
        qt_ref, ka_ref, vt_ref, yc_ref, cu_scr, carry_scr = refs

    @pl.when(pl.program_id(1) == 0)
    def _():
        carry_scr[...] = jnp.broadcast_to(cum0_ref[:, N_META - 1:N_META], carry_scr.shape)
        cu_scr[0:SUBLANES, :] = tail0_ref[...]

    xn = _rms_norm(x_ref[0], gmix_ref[...]).astype(BF16)

    rows = _dot_nt(wrow_ref[...], xn)
    if not meta:
        qt_ref[0] = rows[0:D_ATTN].astype(BF16)
    vt_ref[0] = rows[D_ATTN:2 * D_ATTN].astype(BF16)
    z = rows[2 * D_ATTN:2 * D_ATTN + N_HEADS] + bf_ref[...]
    log_f = jnp.minimum(z, 0.0) - jnp.log1p(jnp.exp(-jnp.abs(z)))
    cum = _split_dot(log_f, tri_ref[...]) + carry_scr[:, 0:1]
    carry_scr[...] = jnp.broadcast_to(cum[:, tile - 1:tile], carry_scr.shape)
    bias = -LOG2E * cum
    if meta:
        cum_ref[...] = cum
        lane = lax.broadcasted_iota(jnp.int32, cum.shape, 1)
        bias = jnp.where(lane < N_META, bias, MASKED)
    bias_cols = _dot(perm_ref[...], _stack_terms(bias)).T

    k = _dot(xn, wtok_ref[:, 0:D_ATTN])
    low = lax.broadcasted_iota(jnp.int32, (1, PAIR), 1) < HEAD_DIM
    for h in range(N_HEADS):
        kp = k[:, (h // 2) * PAIR:(h // 2 + 1) * PAIR]
        ka_ref[0, h] = jnp.where(low == (h % 2 == 0), kp, bias_cols).astype(BF16)

    c0 = D_ATTN
    cu = _dot(xn, wtok_ref[:, c0 + D_CONV:c0 + 2 * D_CONV]) * _dot(xn, wtok_ref[:, c0 + 2 * D_CONV:c0 + 3 * D_CONV])
    cu_scr[SUBLANES:SUBLANES + tile, :] = cu
    if meta:
        tail_ref[...] = cu_scr[N_META:N_META + SUBLANES, :]
        return
    cu1 = cu_scr[SUBLANES - 1:SUBLANES - 1 + tile, :]
    cu2 = cu_scr[SUBLANES - 2:SUBLANES - 2 + tile, :]
    conv = convw_ref[0:1, :] * cu2 + convw_ref[1:2, :] * cu1 + convw_ref[2:3, :] * cu
    cu_scr[0:SUBLANES, :] = cu[tile - SUBLANES:tile, :]
    yc = _dot(xn, wtok_ref[:, c0:c0 + D_CONV]) * conv
    ms = _dot((yc * yc).astype(BF16), gmat_ref[...])
    yc_ref[0] = (yc * lax.rsqrt(ms + EPS) * gconv_ref[...]).astype(BF16)


def _projection(x, gmix, wtok, wrow, bf_col, convw, gconv, gmat, cum0, tail0, *, tile, meta):
    nb, seq, _ = x.shape
    nt = seq // tile
    tri = jnp.triu(jnp.ones((tile, tile), F32)).astype(BF16)
    out_lane = jnp.arange(PAIR) % HEAD_DIM
    src_row = jnp.where(out_lane < BIAS_TERMS * N_HEADS, out_lane, -1)
    perm = (src_row[:, None] == jnp.arange(4 * N_HEADS)[None, :]).astype(BF16)
    const = lambda shape: pl.BlockSpec(shape, lambda b, t: (0,) * len(shape))
    in_specs = [
        pl.BlockSpec((1, tile, D_MODEL), lambda b, t: (b, t, 0)),
        const(gmix.shape), const(wtok.shape), const(wrow.shape), const(bf_col.shape), const(convw.shape),
        const(gconv.shape), const(gmat.shape), const(tri.shape), const(perm.shape), const(cum0.shape),
        const(tail0.shape),
    ]
    tok_spec = pl.BlockSpec((1, tile, D_CONV), lambda b, t: (b, t, 0))
    row_spec = pl.BlockSpec((1, D_ATTN, tile), lambda b, t: (b, 0, t))
    ka_spec = pl.BlockSpec((1, N_HEADS, tile, PAIR), lambda b, t: (b, 0, t, 0))
    tok_shape = jax.ShapeDtypeStruct((nb, seq, D_CONV), BF16)
    row_shape = jax.ShapeDtypeStruct((nb, D_ATTN, seq), BF16)
    ka_shape = jax.ShapeDtypeStruct((nb, N_HEADS, seq, PAIR), BF16)
    if meta:
        out_specs = [ka_spec, row_spec, const((N_HEADS, tile)), const((SUBLANES, D_CONV))]
        out_shape = [ka_shape, row_shape, jax.ShapeDtypeStruct((N_HEADS, tile), F32),
                     jax.ShapeDtypeStruct((SUBLANES, D_CONV), F32)]
    else:
        out_specs = [row_spec, ka_spec, row_spec, tok_spec]
        out_shape = [row_shape, ka_shape, row_shape, tok_shape]
    return pl.pallas_call(
        functools.partial(_proj_kernel, tile=tile, meta=meta),
        grid=(nb, nt),
        in_specs=in_specs,
        out_specs=out_specs,
        out_shape=out_shape,
        scratch_shapes=[pltpu.VMEM((tile + SUBLANES, D_CONV), F32), pltpu.VMEM((N_HEADS, LANES), F32)],
        compiler_params=pltpu.CompilerParams(
            dimension_semantics=("arbitrary", "arbitrary"), vmem_limit_bytes=48 * 1024 * 1024),
        name="proj_meta" if meta else "proj",
    )(x, gmix, wtok, wrow, bf_col, convw, gconv, gmat, tri, perm, cum0, tail0)


def _attn_kernel(qt_ref, ka_ref, vt_ref, kam_ref, vtm_ref, g_ref, o_ref,
                 m_scr, a_scr, acc_scr, s_scr, p_scr, sm_scr, pm_scr):
    i = pl.program_id(1)
    tq, tk = ATTN_TQ, ATTN_TK
    assert tq == 2 * tk, "the pipeline prologue assumes two diagonal key tiles per query block"
    sub = lax.broadcasted_iota(jnp.int32, (HEAD_DIM, tq), 0)
    ones = jnp.ones((BF16_ROWS, max(tk, META_PAD)), BF16)

    def query(h):
        qh = qt_ref[0, h * HEAD_DIM:(h + 1) * HEAD_DIM, :]
        ind = ((sub % N_HEADS == h) & (sub < BIAS_TERMS * N_HEADS)).astype(BF16)
        return jnp.concatenate([qh, ind] if h % 2 == 0 else [ind, qh], axis=0)

    def values(vt, h, k0, width):
        return jnp.concatenate([vt[h * HEAD_DIM:(h + 1) * HEAD_DIM, pl.ds(k0, width)], ones[:, 0:width]], axis=0)

    qs = [query(h) for h in range(N_HEADS)]

    def scores(h, keys, s_buf, width, diag_offset=None):
        s = _dot(keys(h), qs[h])
        if diag_offset is not None:
            key = lax.broadcasted_iota(jnp.int32, (width, tq), 0) + diag_offset
            s = jnp.where(key <= lax.broadcasted_iota(jnp.int32, (width, tq), 1), s, MASKED)
        s_buf[h, 0:width] = s

    def softmax(h, s_buf, p_buf, width, first=False):
        m_new = jnp.max(s_buf[h, 0:width], axis=0, keepdims=True)
        if not first:
            m = m_scr[h]
            m_new = jnp.maximum(m, m_new)
            a_scr[h] = jnp.exp2(m - m_new)
        m_scr[h] = m_new
        p_buf[h, 0:width] = jnp.exp2(s_buf[h, 0:width] - m_new).astype(BF16)

    def weighted(h, vt, p_buf, k0, width, first=False):
        pv = _dot(values(vt, h, k0, width), p_buf[h, 0:width])
        acc_scr[h] = pv if first else a_scr[h] * acc_scr[h] + pv

    heads = range(N_HEADS)
    real_keys = lambda k0: (lambda h: ka_ref[0, h, pl.ds(k0, tk), :])
    real_vt = vt_ref.at[0]
    q0 = pl.multiple_of(i * tq, tq)
    n_full = i * (tq // tk)

    def tile_start(j):
        return pl.multiple_of(jnp.where(j < 2, q0 + j * tk, (j - 2) * tk), tk)

    for h in heads:
        scores(h, lambda h: kam_ref[h], sm_scr, META_PAD)
    for h in heads:
        scores(h, real_keys(q0), s_scr, tk, diag_offset=0)
    for h in heads:
        softmax(h, sm_scr, pm_scr, META_PAD, first=True)
    for h in heads:
        weighted(h, vtm_ref, pm_scr, 0, META_PAD, first=True)
        softmax(h, s_scr, p_scr, tk)
        scores(h, real_keys(q0 + tk), s_scr, tk, diag_offset=tk)

    def body(c, carry):
        k_pv, k_qk = tile_start(c), pl.multiple_of(c * tk, tk)
        for h in heads:
            weighted(h, real_vt, p_scr, k_pv, tk)
            softmax(h, s_scr, p_scr, tk)
            scores(h, real_keys(k_qk), s_scr, tk)
        return carry

    lax.fori_loop(0, n_full, body, 0)
    for h in heads:
        weighted(h, real_vt, p_scr, tile_start(n_full), tk)
        softmax(h, s_scr, p_scr, tk)
    for h in heads:
        weighted(h, real_vt, p_scr, tile_start(n_full + 1), tk)

    def normed(h):
        acc = acc_scr[h]
        o = acc[0:HEAD_DIM] / acc[HEAD_DIM:HEAD_DIM + 1]
        return o * lax.rsqrt(jnp.mean(o * o, axis=0, keepdims=True) + EPS)

    for j in range(N_PAIRS):
        y = jnp.concatenate([normed(2 * j), normed(2 * j + 1)], axis=0).T
        o_ref[0, :, j * PAIR:(j + 1) * PAIR] = (y * g_ref[:, j * PAIR:(j + 1) * PAIR]).astype(BF16)


def _attention(qt, ka, vt, kam, vtm, gattn):
    nb, _, seq = qt.shape
    tq = ATTN_TQ
    whole = lambda a: pl.BlockSpec(a.shape, lambda b, i: (0,) * a.ndim)
    per_batch = lambda a: pl.BlockSpec((1,) + a.shape[1:], lambda b, i: (b,) + (0,) * (a.ndim - 1))
    return pl.pallas_call(
        _attn_kernel,
        grid=(nb, seq // tq),
        in_specs=[
            pl.BlockSpec((1, D_ATTN, tq), lambda b, i: (b, 0, i)),
            per_batch(ka), per_batch(vt), whole(kam), whole(vtm), whole(gattn),
        ],
        out_specs=pl.BlockSpec((1, tq, D_ATTN), lambda b, i: (b, i, 0)),
        out_shape=jax.ShapeDtypeStruct((nb, seq, D_ATTN), BF16),
        scratch_shapes=[
            pltpu.VMEM((N_HEADS, 1, tq), F32), pltpu.VMEM((N_HEADS, 1, tq), F32),
            pltpu.VMEM((N_HEADS, HEAD_DIM + BF16_ROWS, tq), F32),
            pltpu.VMEM((N_HEADS, ATTN_TK, tq), F32), pltpu.VMEM((N_HEADS, ATTN_TK, tq), BF16),
            pltpu.VMEM((N_HEADS, META_PAD, tq), F32), pltpu.VMEM((N_HEADS, META_PAD, tq), BF16),
        ],
        compiler_params=pltpu.CompilerParams(
            dimension_semantics=("arbitrary", "arbitrary"), vmem_limit_bytes=48 * 1024 * 1024),
        name="attn",
    )(qt, ka, vt, kam, vtm, gattn)


def _mlp_kernel(x_ref, ya_ref, yc_ref, wo_ref, gmlp_ref, w1_ref, w2_ref, gfin_ref, o_ref):
    h = (x_ref[0] + _dot(ya_ref[0], wo_ref[0:D_ATTN, :]) + _dot(yc_ref[0], wo_ref[D_ATTN:D_ATTN + D_CONV, :]))
    hn = _rms_norm(h, gmlp_ref[...]).astype(BF16)
    ff = None
    for c in range(0, D_FF, FF_CHUNK):
        a = jnp.square(jnp.maximum(_dot(hn, w1_ref[:, c:c + FF_CHUNK]), 0.0)).astype(BF16)
        part = _dot(a, w2_ref[c:c + FF_CHUNK, :])
        ff = part if ff is None else ff + part
    o_ref[0] = _rms_norm(h + ff, gfin_ref[...])


def _mlp(x, ya, yc, wo, gmlp, w1, w2, gfin):
    nb, seq, _ = x.shape
    t = MLP_TILE
    const = lambda shape: pl.BlockSpec(shape, lambda b, i: (0,) * len(shape), pipeline_mode=pl.Buffered(1))
    return pl.pallas_call(
        _mlp_kernel,
        grid=(nb, seq // t),
        in_specs=[
            pl.BlockSpec((1, t, D_MODEL), lambda b, i: (b, i, 0)),
            pl.BlockSpec((1, t, D_ATTN), lambda b, i: (b, i, 0)),
            pl.BlockSpec((1, t, D_CONV), lambda b, i: (b, i, 0)),
            const(wo.shape), const(gmlp.shape), const(w1.shape), const(w2.shape), const(gfin.shape),
        ],
        out_specs=pl.BlockSpec((1, t, D_MODEL), lambda b, i: (b, i, 0)),
        out_shape=jax.ShapeDtypeStruct((nb, seq, D_MODEL), F32),
        compiler_params=pltpu.CompilerParams(
            dimension_semantics=("parallel", "parallel"), vmem_limit_bytes=56 * 1024 * 1024),
        name="mlp",
    )(x, ya, yc, wo, gmlp, w1, w2, gfin)


def kernel(x, meta_tokens, norm_mix_g, w_in, b_f, conv_w, out_norm_g, w_out, norm_mlp_g, w_ff1, w_ff2,
           final_norm_g):
    assert w_in.shape[0] == 1, "single-layer block"
    w = w_in[0]
    o_q, o_k, o_v, o_f = 0, D_ATTN, 2 * D_ATTN, 3 * D_ATTN
    o_b = o_f + N_HEADS
    wq = w[:, o_q:o_k] * (LOG2E * HEAD_DIM ** -0.5)
    wtok = jnp.concatenate([w[:, o_k:o_v], w[:, o_b:]], axis=1).astype(BF16)
    wrow = jnp.concatenate([wq, w[:, o_v:o_f], w[:, o_f:o_b], jnp.zeros((D_MODEL, SUBLANES), F32)], axis=1)
    wrow = wrow.T.astype(BF16)
    gmix = norm_mix_g[0][None, :]
    bf_col = b_f[0][:, None]
    convw = jnp.concatenate([conv_w[0], jnp.zeros((SUBLANES - CONV_WIDTH, D_CONV), F32)], axis=0)
    gattn = out_norm_g[0, :D_ATTN][None, :]
    gconv = out_norm_g[0, D_ATTN:][None, :]
    group = jnp.arange(D_CONV) // HEAD_DIM
    gmat = ((group[:, None] == group[None, :]).astype(F32) / HEAD_DIM).astype(BF16)

    meta = jnp.zeros((1, META_PAD, D_MODEL), F32).at[0, :N_META].set(meta_tokens)
    proj = functools.partial(_projection, gmix=gmix, wtok=wtok, wrow=wrow, bf_col=bf_col, convw=convw,
                             gconv=gconv, gmat=gmat)
    kam, vtm, cumm, tailm = proj(meta, cum0=jnp.zeros((N_HEADS, LANES), F32),
                                 tail0=jnp.zeros((SUBLANES, D_CONV), F32), tile=META_PAD, meta=True)
    qt, ka, vt, yc = proj(x, cum0=cumm, tail0=tailm, tile=PROJ_TILE, meta=False)

    ya = _attention(qt, ka, vt, kam[0], vtm[0], gattn)

    return _mlp(x, ya, yc, w_out[0].astype(BF16), norm_mlp_g[0][None, :], w_ff1[0].astype(BF16),
                w_ff2[0].astype(BF16), final_norm_g[None, :])
```

```python
import functools

import jax
import jax.numpy as jnp
from jax import lax
from jax.experimental import pallas as pl
from jax.experimental.pallas import tpu as pltpu

D_MODEL = 1024
N_META = 16
HEAD_DIM = 64
D_ATTN = 512
N_HEADS = D_ATTN // HEAD_DIM
D_CONV = 512
CONV_WIDTH = 3
D_FF = 4096
EPS = 1e-5
MASKED = -1e30
LOG2E = 1.4426950408889634
BIAS_TERMS = 3

LANES = 128
SUBLANES = 8
BF16_ROWS = 16
PAIR = 2 * HEAD_DIM
N_PAIRS = N_HEADS // 2
META_PAD = LANES

PROJ_TILE = 512
ATTN_TQ = 512
ATTN_TK = 256
MLP_TILE = 512
FF_CHUNK = 1024

BF16 = jnp.bfloat16
F32 = jnp.float32


def _dot(a, b):
    return jnp.dot(a, b, preferred_element_type=F32)


def _dot_nt(a, b):
    return lax.dot_general(a, b, (((1,), (1,)), ((), ())), preferred_element_type=F32)


def _rms_norm(x, g):
    ms = jnp.mean(x * x, axis=-1, keepdims=True)
    return x * lax.rsqrt(ms + EPS) * g


def _split_bf16(x, parts):
    out = []
    r = x
    for _ in range(parts):
        t = r.astype(BF16)
        out.append(t)
        r = r - t.astype(F32)
    return out


def _stack_terms(x):
    terms = [t.astype(F32) for t in _split_bf16(x, BIAS_TERMS)]
    return jnp.concatenate(terms + [jnp.zeros_like(x)], axis=0).astype(BF16)


def _split_dot(x, w):
    sums = _dot(_stack_terms(x), w)
    h = x.shape[0]
    return sums[0:h] + sums[h:2 * h] + sums[2 * h:3 * h]


def _proj_kernel(x_ref, gmix_ref, wtok_ref, wrow_ref, bf_ref, convw_ref, gconv_ref, gmat_ref, tri_ref, perm_ref,
                 cum0_ref, tail0_ref, *refs, tile, meta):
    if meta:
        ka_ref, vt_ref, cum_ref, tail_ref, cu_scr, carry_scr = refs
    else:
        qt_ref, ka_ref, vt_ref, yc_ref, cu_scr, carry_scr = refs

    @pl.when(pl.program_id(1) == 0)
    def _():
        carry_scr[...] = jnp.broadcast_to(cum0_ref[:, N_META - 1:N_META], carry_scr.shape)
        cu_scr[0:SUBLANES, :] = tail0_ref[...]

    xn = _rms_norm(x_ref[0], gmix_ref[...]).astype(BF16)

    rows = _dot_nt(wrow_ref[...], xn)
    if not meta:
        qt_ref[0] = rows[0:D_ATTN].astype(BF16)
    vt_ref[0] = rows[D_ATTN:2 * D_ATTN].astype(BF16)
    z = rows[2 * D_ATTN:2 * D_ATTN + N_HEADS] + bf_ref[...]
    log_f = jnp.minimum(z, 0.0) - jnp.log1p(jnp.exp(-jnp.abs(z)))
    cum = _split_dot(log_f, tri_ref[...]) + carry_scr[:, 0:1]
    carry_scr[...] = jnp.broadcast_to(cum[:, tile - 1:tile], carry_scr.shape)
    bias = -LOG2E * cum
    if meta:
        cum_ref[...] = cum
        lane = lax.broadcasted_iota(jnp.int32, cum.shape, 1)
        bias = jnp.where(lane < N_META, bias, MASKED)
    bias_cols = _dot(perm_ref[...], _stack_terms(bias)).T

    k = _dot(xn, wtok_ref[:, 0:D_ATTN])
    low = lax.broadcasted_iota(jnp.int32, (1, PAIR), 1) < HEAD_DIM
    for h in range(N_HEADS):
        kp = k[:, (h // 2) * PAIR:(h // 2 + 1) * PAIR]
        ka_ref[0, h] = jnp.where(low == (h % 2 == 0), kp, bias_cols).astype(BF16)

    c0 = D_ATTN
    cu = _dot(xn, wtok_ref[:, c0 + D_CONV:c0 + 2 * D_CONV]) * _dot(xn, wtok_ref[:, c0 + 2 * D_CONV:c0 + 3 * D_CONV])
    cu_scr[SUBLANES:SUBLANES + tile, :] = cu
    if meta:
        tail_ref[...] = cu_scr[N_META:N_META + SUBLANES, :]
        return
    cu1 = cu_scr[SUBLANES - 1:SUBLANES - 1 + tile, :]
    cu2 = cu_scr[SUBLANES - 2:SUBLANES - 2 + tile, :]
    conv = convw_ref[0:1, :] * cu2 + convw_ref[1:2, :] * cu1 + convw_ref[2:3, :] * cu
    cu_scr[0:SUBLANES, :] = cu[tile - SUBLANES:tile, :]
    yc = _dot(xn, wtok_ref[:, c0:c0 + D_CONV]) * conv
    ms = _dot((yc * yc).astype(BF16), gmat_ref[...])
    yc_ref[0] = (yc * lax.rsqrt(ms + EPS) * gconv_ref[...]).astype(BF16)


def _projection(x, gmix, wtok, wrow, bf_col, convw, gconv, gmat, cum0, tail0, *, tile, meta):
    nb, seq, _ = x.shape
    nt = seq // tile
    tri = jnp.triu(jnp.ones((tile, tile), F32)).astype(BF16)
    out_lane = jnp.arange(PAIR) % HEAD_DIM
    src_row = jnp.where(out_lane < BIAS_TERMS * N_HEADS, out_lane, -1)
    perm = (src_row[:, None] == jnp.arange(4 * N_HEADS)[None, :]).astype(BF16)
    const = lambda shape: pl.BlockSpec(shape, lambda b, t: (0,) * len(shape))
    in_specs = [
        pl.BlockSpec((1, tile, D_MODEL), lambda b, t: (b, t, 0)),
        const(gmix.shape), const(wtok.shape), const(wrow.shape), const(bf_col.shape), const(convw.shape),
        const(gconv.shape), const(gmat.shape), const(tri.shape), const(perm.shape), const(cum0.shape),
        const(tail0.shape),
    ]
    tok_spec = pl.BlockSpec((1, tile, D_CONV), lambda b, t: (b, t, 0))
    row_spec = pl.BlockSpec((1, D_ATTN, tile), lambda b, t: (b, 0, t))
    ka_spec = pl.BlockSpec((1, N_HEADS, tile, PAIR), lambda b, t: (b, 0, t, 0))
    tok_shape = jax.ShapeDtypeStruct((nb, seq, D_CONV), BF16)
    row_shape = jax.ShapeDtypeStruct((nb, D_ATTN, seq), BF16)
    ka_shape = jax.ShapeDtypeStruct((nb, N_HEADS, seq, PAIR), BF16)
    if meta:
        out_specs = [ka_spec, row_spec, const((N_HEADS, tile)), const((SUBLANES, D_CONV))]
        out_shape = [ka_shape, row_shape, jax.ShapeDtypeStruct((N_HEADS, tile), F32),
                     jax.ShapeDtypeStruct((SUBLANES, D_CONV), F32)]
    else:
        out_specs = [row_spec, ka_spec, row_spec, tok_spec]
        out_shape = [row_shape, ka_shape, row_shape, tok_shape]
    return pl.pallas_call(
        functools.partial(_proj_kernel, tile=tile, meta=meta),
        grid=(nb, nt),
        in_specs=in_specs,
        out_specs=out_specs,
        out_shape=out_shape,
        scratch_shapes=[pltpu.VMEM((tile + SUBLANES, D_CONV), F32), pltpu.VMEM((N_HEADS, LANES), F32)],
        compiler_params=pltpu.CompilerParams(
            dimension_semantics=("arbitrary", "arbitrary"), vmem_limit_bytes=48 * 1024 * 1024),
        name="proj_meta" if meta else "proj",
    )(x, gmix, wtok, wrow, bf_col, convw, gconv, gmat, tri, perm, cum0, tail0)


def _attn_kernel(qt_ref, ka_ref, vt_ref, kam_ref, vtm_ref, g_ref, o_ref,
                 m_scr, a_scr, acc_scr, s_scr, p_scr, sm_scr, pm_scr):
    i = pl.program_id(1)
    tq, tk = ATTN_TQ, ATTN_TK
    assert tq == 2 * tk, "the pipeline prologue assumes two diagonal key tiles per query block"
    sub = lax.broadcasted_iota(jnp.int32, (HEAD_DIM, tq), 0)
    ones = jnp.ones((BF16_ROWS, max(tk, META_PAD)), BF16)

    def query(h):
        qh = qt_ref[0, h * HEAD_DIM:(h + 1) * HEAD_DIM, :]
        ind = ((sub % N_HEADS == h) & (sub < BIAS_TERMS * N_HEADS)).astype(BF16)
        return jnp.concatenate([qh, ind] if h % 2 == 0 else [ind, qh], axis=0)

    def values(vt, h, k0, width):
        return jnp.concatenate([vt[h * HEAD_DIM:(h + 1) * HEAD_DIM, pl.ds(k0, width)], ones[:, 0:width]], axis=0)

    qs = [query(h) for h in range(N_HEADS)]

    def scores(h, keys, s_buf, width, diag_offset=None):
        lo = 0 if diag_offset is None else diag_offset
        s = _dot(keys(h), qs[h][:, lo:tq])
        if diag_offset is not None:
            key = lax.broadcasted_iota(jnp.int32, s.shape, 0)
            s = jnp.where(key <= lax.broadcasted_iota(jnp.int32, s.shape, 1), s, MASKED)
            if lo:
                s_buf[h, 0:width, 0:lo] = jnp.full((width, lo), MASKED, F32)
        s_buf[h, 0:width, lo:tq] = s

    def softmax(h, s_buf, p_buf, width, first=False):
        m_new = jnp.max(s_buf[h, 0:width], axis=0, keepdims=True)
        if not first:
            m = m_scr[h]
            m_new = jnp.maximum(m, m_new)
            a_scr[h] = jnp.exp2(m - m_new)
        m_scr[h] = m_new
        p_buf[h, 0:width] = jnp.exp2(s_buf[h, 0:width] - m_new).astype(BF16)

    def weighted(h, vt, p_buf, k0, width, first=False):
        pv = _dot(values(vt, h, k0, width), p_buf[h, 0:width])
        acc_scr[h] = pv if first else a_scr[h] * acc_scr[h] + pv

    heads = range(N_HEADS)
    real_keys = lambda k0: (lambda h: ka_ref[0, h, pl.ds(k0, tk), :])
    real_vt = vt_ref.at[0]
    q0 = pl.multiple_of(i * tq, tq)
    n_full = i * (tq // tk)

    def tile_start(j):
        return pl.multiple_of(jnp.where(j < 2, q0 + j * tk, (j - 2) * tk), tk)

    for h in heads:
        scores(h, lambda h: kam_ref[h, 0:N_META, :], sm_scr, N_META)
    for h in heads:
        scores(h, real_keys(q0), s_scr, tk, diag_offset=0)
    for h in heads:
        softmax(h, sm_scr, pm_scr, N_META, first=True)
    for h in heads:
        weighted(h, vtm_ref, pm_scr, 0, N_META, first=True)
        softmax(h, s_scr, p_scr, tk)
        scores(h, real_keys(q0 + tk), s_scr, tk, diag_offset=tk)

    def trip(c):
        k_pv, k_qk = tile_start(c), pl.multiple_of(c * tk, tk)
        for h in heads:
            weighted(h, real_vt, p_scr, k_pv, tk)
            softmax(h, s_scr, p_scr, tk)
            scores(h, real_keys(k_qk), s_scr, tk)

    def body(c, carry):
        for u in range(tq // tk):
            trip((tq // tk) * c + u)
        return carry

    lax.fori_loop(0, i, body, 0)
    for h in heads:
        weighted(h, real_vt, p_scr, tile_start(n_full), tk)
        softmax(h, s_scr, p_scr, tk)
    for h in heads:
        weighted(h, real_vt, p_scr, tile_start(n_full + 1), tk)

    def normed(h):
        acc = acc_scr[h]
        o = acc[0:HEAD_DIM] / acc[HEAD_DIM:HEAD_DIM + 1]
        return o * lax.rsqrt(jnp.mean(o * o, axis=0, keepdims=True) + EPS)

    for j in range(N_PAIRS):
        y = jnp.concatenate([normed(2 * j), normed(2 * j + 1)], axis=0).T
        o_ref[0, :, j * PAIR:(j + 1) * PAIR] = (y * g_ref[:, j * PAIR:(j + 1) * PAIR]).astype(BF16)


def _attention(qt, ka, vt, kam, vtm, gattn):
    nb, _, seq = qt.shape
    tq = ATTN_TQ
    whole = lambda a: pl.BlockSpec(a.shape, lambda b, i: (0,) * a.ndim)
    per_batch = lambda a: pl.BlockSpec((1,) + a.shape[1:], lambda b, i: (b,) + (0,) * (a.ndim - 1))
    return pl.pallas_call(
        _attn_kernel,
        grid=(nb, seq // tq),
        in_specs=[
            pl.BlockSpec((1, D_ATTN, tq), lambda b, i: (b, 0, i)),
            per_batch(ka), per_batch(vt), whole(kam), whole(vtm), whole(gattn),
        ],
        out_specs=pl.BlockSpec((1, tq, D_ATTN), lambda b, i: (b, i, 0)),
        out_shape=jax.ShapeDtypeStruct((nb, seq, D_ATTN), BF16),
        scratch_shapes=[
            pltpu.VMEM((N_HEADS, 1, tq), F32), pltpu.VMEM((N_HEADS, 1, tq), F32),
            pltpu.VMEM((N_HEADS, HEAD_DIM + BF16_ROWS, tq), F32),
            pltpu.VMEM((N_HEADS, ATTN_TK, tq), F32), pltpu.VMEM((N_HEADS, ATTN_TK, tq), BF16),
            pltpu.VMEM((N_HEADS, N_META, tq), F32), pltpu.VMEM((N_HEADS, N_META, tq), BF16),
        ],
        compiler_params=pltpu.CompilerParams(
            dimension_semantics=("arbitrary", "arbitrary"), vmem_limit_bytes=48 * 1024 * 1024),
        name="attn",
    )(qt, ka, vt, kam, vtm, gattn)


def _mlp_kernel(x_ref, ya_ref, yc_ref, wo_ref, gmlp_ref, w1_ref, w2_ref, gfin_ref, o_ref):
    h = (x_ref[0] + _dot(ya_ref[0], wo_ref[0:D_ATTN, :]) + _dot(yc_ref[0], wo_ref[D_ATTN:D_ATTN + D_CONV, :]))
    hn = _rms_norm(h, gmlp_ref[...]).astype(BF16)
    ff = None
    for c in range(0, D_FF, FF_CHUNK):
        a = jnp.square(jnp.maximum(_dot(hn, w1_ref[:, c:c + FF_CHUNK]), 0.0)).astype(BF16)
        part = _dot(a, w2_ref[c:c + FF_CHUNK, :])
        ff = part if ff is None else ff + part
    o_ref[0] = _rms_norm(h + ff, gfin_ref[...])


def _mlp(x, ya, yc, wo, gmlp, w1, w2, gfin):
    nb, seq, _ = x.shape
    t = MLP_TILE
    const = lambda shape: pl.BlockSpec(shape, lambda b, i: (0,) * len(shape), pipeline_mode=pl.Buffered(1))
    return pl.pallas_call(
        _mlp_kernel,
        grid=(nb, seq // t),
        in_specs=[
            pl.BlockSpec((1, t, D_MODEL), lambda b, i: (b, i, 0)),
            pl.BlockSpec((1, t, D_ATTN), lambda b, i: (b, i, 0)),
            pl.BlockSpec((1, t, D_CONV), lambda b, i: (b, i, 0)),
            const(wo.shape), const(gmlp.shape), const(w1.shape), const(w2.shape), const(gfin.shape),
        ],
        out_specs=pl.BlockSpec((1, t, D_MODEL), lambda b, i: (b, i, 0)),
        out_shape=jax.ShapeDtypeStruct((nb, seq, D_MODEL), F32),
        compiler_params=pltpu.CompilerParams(
            dimension_semantics=("parallel", "parallel"), vmem_limit_bytes=56 * 1024 * 1024),
        name="mlp",
    )(x, ya, yc, wo, gmlp, w1, w2, gfin)


def kernel(x, meta_tokens, norm_mix_g, w_in, b_f, conv_w, out_norm_g, w_out, norm_mlp_g, w_ff1, w_ff2,
           final_norm_g):
    assert w_in.shape[0] == 1, "single-layer block"
    w = w_in[0]
    o_q, o_k, o_v, o_f = 0, D_ATTN, 2 * D_ATTN, 3 * D_ATTN
    o_b = o_f + N_HEADS
    wq = w[:, o_q:o_k] * (LOG2E * HEAD_DIM ** -0.5)
    wtok = jnp.concatenate([w[:, o_k:o_v], w[:, o_b:]], axis=1).astype(BF16)
    wrow = jnp.concatenate([wq, w[:, o_v:o_f], w[:, o_f:o_b], jnp.zeros((D_MODEL, SUBLANES), F32)], axis=1)
    wrow = wrow.T.astype(BF16)
    gmix = norm_mix_g[0][None, :]
    bf_col = b_f[0][:, None]
    convw = jnp.concatenate([conv_w[0], jnp.zeros((SUBLANES - CONV_WIDTH, D_CONV), F32)], axis=0)
    gattn = out_norm_g[0, :D_ATTN][None, :]
    gconv = out_norm_g[0, D_ATTN:][None, :]
    group = jnp.arange(D_CONV) // HEAD_DIM
    gmat = ((group[:, None] == group[None, :]).astype(F32) / HEAD_DIM).astype(BF16)

    meta = jnp.zeros((1, META_PAD, D_MODEL), F32).at[0, :N_META].set(meta_tokens)
    proj = functools.partial(_projection, gmix=gmix, wtok=wtok, wrow=wrow, bf_col=bf_col, convw=convw,
                             gconv=gconv, gmat=gmat)
    kam, vtm, cumm, tailm = proj(meta, cum0=jnp.zeros((N_HEADS, LANES), F32),
                                 tail0=jnp.zeros((SUBLANES, D_CONV), F32), tile=META_PAD, meta=True)
    qt, ka, vt, yc = proj(x, cum0=cumm, tail0=tailm, tile=PROJ_TILE, meta=False)

    ya = _attention(qt, ka, vt, kam[0], vtm[0], gattn)

    return _mlp(x, ya, yc, w_out[0].astype(BF16), norm_mlp_g[0][None, :], w_ff1[0].astype(BF16),
                w_ff2[0].astype(BF16), final_norm_g[None, :])
```

```python
import functools

import jax
import jax.numpy as jnp
from jax import lax
from jax.experimental import pallas as pl
from jax.experimental.pallas import tpu as pltpu

D_MODEL = 1024
N_META = 16
HEAD_DIM = 64
D_ATTN = 512
N_HEADS = D_ATTN // HEAD_DIM
D_CONV = 512
CONV_WIDTH = 3
D_FF = 4096
EPS = 1e-5
MASKED = -1e30
LOG2E = 1.4426950408889634
BIAS_TERMS = 3

LANES = 128
SUBLANES = 8
BF16_ROWS = 16
PAIR = 2 * HEAD_DIM
N_PAIRS = N_HEADS // 2
META_PAD = LANES

PROJ_TILE = 1024
PROJ_SUB = 512
ATTN_TQ = 512
ATTN_TK = 256
MLP_TILE = 512
FF_CHUNK = 1024

BF16 = jnp.bfloat16
F32 = jnp.float32


def _dot(a, b):
    return jnp.dot(a, b, preferred_element_type=F32)


def _dot_nt(a, b):
    return lax.dot_general(a, b, (((1,), (1,)), ((), ())), preferred_element_type=F32)


def _rms_norm(x, g):
    ms = jnp.mean(x * x, axis=-1, keepdims=True)
    return x * lax.rsqrt(ms + EPS) * g


def _split_bf16(x, parts):
    out = []
    r = x
    for _ in range(parts):
        t = r.astype(BF16)
        out.append(t)
        r = r - t.astype(F32)
    return out


def _stack_terms(x):
    terms = [t.astype(F32) for t in _split_bf16(x, BIAS_TERMS)]
    return jnp.concatenate(terms + [jnp.zeros_like(x)], axis=0).astype(BF16)


def _split_dot(x, w):
    sums = _dot(_stack_terms(x), w)
    h = x.shape[0]
    return sums[0:h] + sums[h:2 * h] + sums[2 * h:3 * h]


def _proj_kernel(x_ref, gmix_ref, wtok_ref, wrow_ref, bf_ref, convw_ref, gconv_ref, gmat_ref, tri_ref, perm_ref,
                 cum0_ref, tail0_ref, *refs, tile, sub, meta):
    if meta:
        ka_ref, vt_ref, cum_ref, tail_ref, cu_scr, carry_scr = refs
    else:
        qt_ref, ka_ref, vt_ref, yc_ref, cu_scr, carry_scr = refs

    @pl.when(pl.program_id(1) == 0)
    def _():
        carry_scr[...] = jnp.broadcast_to(cum0_ref[:, N_META - 1:N_META], carry_scr.shape)
        cu_scr[0:SUBLANES, :] = tail0_ref[...]

    def rows_at(r0):
        xn = _rms_norm(x_ref[0, r0:r0 + sub], gmix_ref[...]).astype(BF16)

        rows = _dot_nt(wrow_ref[...], xn)
        if not meta:
            qt_ref[0, :, r0:r0 + sub] = rows[0:D_ATTN].astype(BF16)
        vt_ref[0, :, r0:r0 + sub] = rows[D_ATTN:2 * D_ATTN].astype(BF16)
        z = rows[2 * D_ATTN:2 * D_ATTN + N_HEADS] + bf_ref[...]
        log_f = jnp.minimum(z, 0.0) - jnp.log1p(jnp.exp(-jnp.abs(z)))

        c0 = D_ATTN
        cu = (_dot(xn, wtok_ref[:, c0 + D_CONV:c0 + 2 * D_CONV])
              * _dot(xn, wtok_ref[:, c0 + 2 * D_CONV:c0 + 3 * D_CONV]))
        cu_scr[SUBLANES + r0:SUBLANES + r0 + sub, :] = cu

        cum = _split_dot(log_f, tri_ref[...]) + carry_scr[:, 0:1]
        carry_scr[...] = jnp.broadcast_to(cum[:, sub - 1:sub], carry_scr.shape)
        bias = -LOG2E * cum
        if meta:
            cum_ref[...] = cum
            lane = lax.broadcasted_iota(jnp.int32, cum.shape, 1)
            bias = jnp.where(lane < N_META, bias, MASKED)
            tail_ref[...] = cu_scr[N_META:N_META + SUBLANES, :]
        else:
            cu1 = cu_scr[SUBLANES + r0 - 1:SUBLANES + r0 - 1 + sub, :]
            cu2 = cu_scr[SUBLANES + r0 - 2:SUBLANES + r0 - 2 + sub, :]
            conv = convw_ref[0:1, :] * cu2 + convw_ref[1:2, :] * cu1 + convw_ref[2:3, :] * cu
            yc = _dot(xn, wtok_ref[:, c0:c0 + D_CONV]) * conv
        bias_cols = _dot(perm_ref[...], _stack_terms(bias)).T

        low = lax.broadcasted_iota(jnp.int32, (1, PAIR), 1) < HEAD_DIM

        def keys(pairs):
            k = _dot(xn, wtok_ref[:, pairs[0] * PAIR:(pairs[-1] + 1) * PAIR])
            for h in range(2 * pairs[0], 2 * pairs[-1] + 2):
                kp = k[:, (h // 2 - pairs[0]) * PAIR:(h // 2 - pairs[0] + 1) * PAIR]
                ka_ref[0, h, r0:r0 + sub] = jnp.where(low == (h % 2 == 0), kp, bias_cols).astype(BF16)

        keys((0, 1))
        if not meta:
            ms = _dot((yc * yc).astype(BF16), gmat_ref[...])
            yc_ref[0, r0:r0 + sub] = (yc * lax.rsqrt(ms + EPS) * gconv_ref[...]).astype(BF16)
        keys((2, 3))

    for r0 in range(0, tile, sub):
        rows_at(r0)
    if not meta:
        cu_scr[0:SUBLANES, :] = cu_scr[tile:tile + SUBLANES, :]


def _projection(x, gmix, wtok, wrow, bf_col, convw, gconv, gmat, cum0, tail0, *, tile, sub, meta):
    nb, seq, _ = x.shape
    nt = seq // tile
    tri = jnp.triu(jnp.ones((sub, sub), F32)).astype(BF16)
    out_lane = jnp.arange(PAIR) % HEAD_DIM
    src_row = jnp.where(out_lane < BIAS_TERMS * N_HEADS, out_lane, -1)
    perm = (src_row[:, None] == jnp.arange(4 * N_HEADS)[None, :]).astype(BF16)
    const = lambda shape: pl.BlockSpec(shape, lambda b, t: (0,) * len(shape))
    in_specs = [
        pl.BlockSpec((1, tile, D_MODEL), lambda b, t: (b, t, 0)),
        const(gmix.shape), const(wtok.shape), const(wrow.shape), const(bf_col.shape), const(convw.shape),
        const(gconv.shape), const(gmat.shape), const(tri.shape), const(perm.shape), const(cum0.shape),
        const(tail0.shape),
    ]
    tok_spec = pl.BlockSpec((1, tile, D_CONV), lambda b, t: (b, t, 0))
    row_spec = pl.BlockSpec((1, D_ATTN, tile), lambda b, t: (b, 0, t))
    ka_spec = pl.BlockSpec((1, N_HEADS, tile, PAIR), lambda b, t: (b, 0, t, 0))
    tok_shape = jax.ShapeDtypeStruct((nb, seq, D_CONV), BF16)
    row_shape = jax.ShapeDtypeStruct((nb, D_ATTN, seq), BF16)
    ka_shape = jax.ShapeDtypeStruct((nb, N_HEADS, seq, PAIR), BF16)
    if meta:
        out_specs = [ka_spec, row_spec, const((N_HEADS, tile)), const((SUBLANES, D_CONV))]
        out_shape = [ka_shape, row_shape, jax.ShapeDtypeStruct((N_HEADS, tile), F32),
                     jax.ShapeDtypeStruct((SUBLANES, D_CONV), F32)]
    else:
        out_specs = [row_spec, ka_spec, row_spec, tok_spec]
        out_shape = [row_shape, ka_shape, row_shape, tok_shape]
    return pl.pallas_call(
        functools.partial(_proj_kernel, tile=tile, sub=sub, meta=meta),
        grid=(nb, nt),
        in_specs=in_specs,
        out_specs=out_specs,
        out_shape=out_shape,
        scratch_shapes=[pltpu.VMEM((tile + SUBLANES, D_CONV), F32), pltpu.VMEM((N_HEADS, LANES), F32)],
        compiler_params=pltpu.CompilerParams(
            dimension_semantics=("arbitrary", "arbitrary"), vmem_limit_bytes=48 * 1024 * 1024),
        name="proj_meta" if meta else "proj",
    )(x, gmix, wtok, wrow, bf_col, convw, gconv, gmat, tri, perm, cum0, tail0)


def _attn_kernel(qt_ref, ka_ref, vt_ref, kam_ref, vtm_ref, g_ref, o_ref,
                 m_scr, a_scr, acc_scr, s_scr, p_scr, sm_scr, pm_scr):
    i = pl.program_id(1)
    tq, tk = ATTN_TQ, ATTN_TK
    assert tq == 2 * tk, "the pipeline prologue assumes two diagonal key tiles per query block"
    sub = lax.broadcasted_iota(jnp.int32, (HEAD_DIM, tq), 0)
    ones = jnp.ones((BF16_ROWS, max(tk, META_PAD)), BF16)

    def query(h):
        qh = qt_ref[0, h * HEAD_DIM:(h + 1) * HEAD_DIM, :]
        ind = ((sub % N_HEADS == h) & (sub < BIAS_TERMS * N_HEADS)).astype(BF16)
        return jnp.concatenate([qh, ind] if h % 2 == 0 else [ind, qh], axis=0)

    def values(vt, h, k0, width):
        return jnp.concatenate([vt[h * HEAD_DIM:(h + 1) * HEAD_DIM, pl.ds(k0, width)], ones[:, 0:width]], axis=0)

    qs = [query(h) for h in range(N_HEADS)]

    def scores(h, keys, s_buf, width, diag_offset=None):
        lo = 0 if diag_offset is None else diag_offset
        s = _dot(keys(h), qs[h][:, lo:tq])
        if diag_offset is not None:
            key = lax.broadcasted_iota(jnp.int32, s.shape, 0)
            s = jnp.where(key <= lax.broadcasted_iota(jnp.int32, s.shape, 1), s, MASKED)
            if lo:
                s_buf[h, 0:width, 0:lo] = jnp.full((width, lo), MASKED, F32)
        s_buf[h, 0:width, lo:tq] = s

    def softmax(h, s_buf, p_buf, width, first=False):
        m_new = jnp.max(s_buf[h, 0:width], axis=0, keepdims=True)
        if not first:
            m = m_scr[h]
            m_new = jnp.maximum(m, m_new)
            a_scr[h] = jnp.exp2(m - m_new)
        m_scr[h] = m_new
        p_buf[h, 0:width] = jnp.exp2(s_buf[h, 0:width] - m_new).astype(BF16)

    def weighted(h, vt, p_buf, k0, width, first=False):
        pv = _dot(values(vt, h, k0, width), p_buf[h, 0:width])
        acc_scr[h] = pv if first else a_scr[h] * acc_scr[h] + pv

    heads = range(N_HEADS)
    real_keys = lambda k0: (lambda h: ka_ref[0, h, pl.ds(k0, tk), :])
    real_vt = vt_ref.at[0]
    q0 = pl.multiple_of(i * tq, tq)
    n_full = i * (tq // tk)

    def tile_start(j):
        return pl.multiple_of(jnp.where(j < 2, q0 + j * tk, (j - 2) * tk), tk)

    for h in heads:
        scores(h, lambda h: kam_ref[h, 0:N_META, :], sm_scr, N_META)
    for h in heads:
        scores(h, real_keys(q0), s_scr, tk, diag_offset=0)
    for h in heads:
        softmax(h, sm_scr, pm_scr, N_META, first=True)
    for h in heads:
        weighted(h, vtm_ref, pm_scr, 0, N_META, first=True)
        softmax(h, s_scr, p_scr, tk)
        scores(h, real_keys(q0 + tk), s_scr, tk, diag_offset=tk)

    def trip(c):
        k_pv, k_qk = tile_start(c), pl.multiple_of(c * tk, tk)
        for h in heads:
            weighted(h, real_vt, p_scr, k_pv, tk)
            softmax(h, s_scr, p_scr, tk)
            scores(h, real_keys(k_qk), s_scr, tk)

    def body(c, carry):
        for u in range(tq // tk):
            trip((tq // tk) * c + u)
        return carry

    lax.fori_loop(0, i, body, 0)
    for h in heads:
        weighted(h, real_vt, p_scr, tile_start(n_full), tk)
        softmax(h, s_scr, p_scr, tk)
    for h in heads:
        weighted(h, real_vt, p_scr, tile_start(n_full + 1), tk)

    def normed(h):
        acc = acc_scr[h]
        o = acc[0:HEAD_DIM] / acc[HEAD_DIM:HEAD_DIM + 1]
        return o * lax.rsqrt(jnp.mean(o * o, axis=0, keepdims=True) + EPS)

    for j in range(N_PAIRS):
        y = jnp.concatenate([normed(2 * j), normed(2 * j + 1)], axis=0).T
        o_ref[0, :, j * PAIR:(j + 1) * PAIR] = (y * g_ref[:, j * PAIR:(j + 1) * PAIR]).astype(BF16)


def _attention(qt, ka, vt, kam, vtm, gattn):
    nb, _, seq = qt.shape
    tq = ATTN_TQ
    whole = lambda a: pl.BlockSpec(a.shape, lambda b, i: (0,) * a.ndim)
    per_batch = lambda a: pl.BlockSpec((1,) + a.shape[1:], lambda b, i: (b,) + (0,) * (a.ndim - 1))
    return pl.pallas_call(
        _attn_kernel,
        grid=(nb, seq // tq),
        in_specs=[
            pl.BlockSpec((1, D_ATTN, tq), lambda b, i: (b, 0, i)),
            per_batch(ka), per_batch(vt), whole(kam), whole(vtm), whole(gattn),
        ],
        out_specs=pl.BlockSpec((1, tq, D_ATTN), lambda b, i: (b, i, 0)),
        out_shape=jax.ShapeDtypeStruct((nb, seq, D_ATTN), BF16),
        scratch_shapes=[
            pltpu.VMEM((N_HEADS, 1, tq), F32), pltpu.VMEM((N_HEADS, 1, tq), F32),
            pltpu.VMEM((N_HEADS, HEAD_DIM + BF16_ROWS, tq), F32),
            pltpu.VMEM((N_HEADS, ATTN_TK, tq), F32), pltpu.VMEM((N_HEADS, ATTN_TK, tq), BF16),
            pltpu.VMEM((N_HEADS, N_META, tq), F32), pltpu.VMEM((N_HEADS, N_META, tq), BF16),
        ],
        compiler_params=pltpu.CompilerParams(
            dimension_semantics=("arbitrary", "arbitrary"), vmem_limit_bytes=48 * 1024 * 1024),
        name="attn",
    )(qt, ka, vt, kam, vtm, gattn)


def _mlp_kernel(x_ref, ya_ref, yc_ref, wo_ref, gmlp_ref, w1_ref, w2_ref, gfin_ref, o_ref):
    h = (x_ref[0] + _dot(ya_ref[0], wo_ref[0:D_ATTN, :]) + _dot(yc_ref[0], wo_ref[D_ATTN:D_ATTN + D_CONV, :]))
    hn = _rms_norm(h, gmlp_ref[...]).astype(BF16)
    ff = None
    for c in range(0, D_FF, FF_CHUNK):
        a = jnp.square(jnp.maximum(_dot(hn, w1_ref[:, c:c + FF_CHUNK]), 0.0)).astype(BF16)
        part = _dot(a, w2_ref[c:c + FF_CHUNK, :])
        ff = part if ff is None else ff + part
    o_ref[0] = _rms_norm(h + ff, gfin_ref[...])


def _mlp(x, ya, yc, wo, gmlp, w1, w2, gfin):
    nb, seq, _ = x.shape
    t = MLP_TILE
    const = lambda shape: pl.BlockSpec(shape, lambda b, i: (0,) * len(shape), pipeline_mode=pl.Buffered(1))
    return pl.pallas_call(
        _mlp_kernel,
        grid=(nb, seq // t),
        in_specs=[
            pl.BlockSpec((1, t, D_MODEL), lambda b, i: (b, i, 0)),
            pl.BlockSpec((1, t, D_ATTN), lambda b, i: (b, i, 0)),
            pl.BlockSpec((1, t, D_CONV), lambda b, i: (b, i, 0)),
            const(wo.shape), const(gmlp.shape), const(w1.shape), const(w2.shape), const(gfin.shape),
        ],
        out_specs=pl.BlockSpec((1, t, D_MODEL), lambda b, i: (b, i, 0)),
        out_shape=jax.ShapeDtypeStruct((nb, seq, D_MODEL), F32),
        compiler_params=pltpu.CompilerParams(
            dimension_semantics=("parallel", "parallel"), vmem_limit_bytes=56 * 1024 * 1024),
        name="mlp",
    )(x, ya, yc, wo, gmlp, w1, w2, gfin)


def kernel(x, meta_tokens, norm_mix_g, w_in, b_f, conv_w, out_norm_g, w_out, norm_mlp_g, w_ff1, w_ff2,
           final_norm_g):
    assert w_in.shape[0] == 1, "single-layer block"
    w = w_in[0]
    o_q, o_k, o_v, o_f = 0, D_ATTN, 2 * D_ATTN, 3 * D_ATTN
    o_b = o_f + N_HEADS
    wq = w[:, o_q:o_k] * (LOG2E * HEAD_DIM ** -0.5)
    wtok = jnp.concatenate([w[:, o_k:o_v], w[:, o_b:]], axis=1).astype(BF16)
    wrow = jnp.concatenate([wq, w[:, o_v:o_f], w[:, o_f:o_b], jnp.zeros((D_MODEL, SUBLANES), F32)], axis=1)
    wrow = wrow.T.astype(BF16)
    gmix = norm_mix_g[0][None, :]
    bf_col = b_f[0][:, None]
    convw = jnp.concatenate([conv_w[0], jnp.zeros((SUBLANES - CONV_WIDTH, D_CONV), F32)], axis=0)
    gattn = out_norm_g[0, :D_ATTN][None, :]
    gconv = out_norm_g[0, D_ATTN:][None, :]
    group = jnp.arange(D_CONV) // HEAD_DIM
    gmat = ((group[:, None] == group[None, :]).astype(F32) / HEAD_DIM).astype(BF16)

    meta = jnp.zeros((1, META_PAD, D_MODEL), F32).at[0, :N_META].set(meta_tokens)
    proj = functools.partial(_projection, gmix=gmix, wtok=wtok, wrow=wrow, bf_col=bf_col, convw=convw,
                             gconv=gconv, gmat=gmat)
    kam, vtm, cumm, tailm = proj(meta, cum0=jnp.zeros((N_HEADS, LANES), F32),
                                 tail0=jnp.zeros((SUBLANES, D_CONV), F32), tile=META_PAD, sub=META_PAD,
                                 meta=True)
    qt, ka, vt, yc = proj(x, cum0=cumm, tail0=tailm, tile=PROJ_TILE, sub=PROJ_SUB, meta=False)

    ya = _attention(qt, ka, vt, kam[0], vtm[0], gattn)

    return _mlp(x, ya, yc, w_out[0].astype(BF16), norm_mlp_g[0][None, :], w_ff1[0].astype(BF16),
                w_ff2[0].astype(BF16), final_norm_g[None, :])
```

```python
import functools

import jax
import jax.numpy as jnp
from jax import lax
from jax.experimental import pallas as pl
from jax.experimental.pallas import tpu as pltpu

D_MODEL = 1024
N_META = 16
HEAD_DIM = 64
D_ATTN = 512
N_HEADS = D_ATTN // HEAD_DIM
D_CONV = 512
CONV_WIDTH = 3
D_FF = 4096
EPS = 1e-5
MASKED = -1e30
LOG2E = 1.4426950408889634
BIAS_TERMS = 3

LANES = 128
SUBLANES = 8
BF16_ROWS = 16
PAIR = 2 * HEAD_DIM
N_PAIRS = N_HEADS // 2
META_PAD = LANES

PROJ_TILE = 1024
PROJ_SUB = 512
ATTN_TQ = 512
ATTN_TK = 256
MLP_TILE = 512
FF_CHUNK = 1024

BF16 = jnp.bfloat16
F32 = jnp.float32


def _dot(a, b):
    return jnp.dot(a, b, preferred_element_type=F32)


def _dot_nt(a, b):
    return lax.dot_general(a, b, (((1,), (1,)), ((), ())), preferred_element_type=F32)


def _rms_norm(x, g):
    ms = jnp.mean(x * x, axis=-1, keepdims=True)
    return x * lax.rsqrt(ms + EPS) * g


def _split_bf16(x, parts):
    out = []
    r = x
    for _ in range(parts):
        t = r.astype(BF16)
        out.append(t)
        r = r - t.astype(F32)
    return out


def _stack_terms(x):
    terms = [t.astype(F32) for t in _split_bf16(x, BIAS_TERMS)]
    return jnp.concatenate(terms + [jnp.zeros_like(x)], axis=0).astype(BF16)


def _split_dot(x, w):
    sums = _dot(_stack_terms(x), w)
    h = x.shape[0]
    return sums[0:h] + sums[h:2 * h] + sums[2 * h:3 * h]


def _proj_kernel(x_ref, gmix_ref, wtok_ref, wrow_ref, bf_ref, convw_ref, gconv_ref, gmat_ref, tri_ref, perm_ref,
                 cum0_ref, tail0_ref, *refs, tile, sub, meta):
    if meta:
        ka_ref, vt_ref, cum_ref, tail_ref, cu_scr, carry_scr = refs
    else:
        qt_ref, ka_ref, vt_ref, yc_ref, cu_scr, carry_scr = refs

    @pl.when(pl.program_id(1) == 0)
    def _():
        carry_scr[...] = jnp.broadcast_to(cum0_ref[:, N_META - 1:N_META], carry_scr.shape)
        cu_scr[0:SUBLANES, :] = tail0_ref[...]

    def rows_at(r0):
        xn = _rms_norm(x_ref[0, r0:r0 + sub], gmix_ref[...]).astype(BF16)

        rows = _dot_nt(wrow_ref[...], xn)
        if not meta:
            qt_ref[0, :, r0:r0 + sub] = rows[0:D_ATTN].astype(BF16)
        vt_ref[0, :, r0:r0 + sub] = rows[D_ATTN:2 * D_ATTN].astype(BF16)
        z = rows[2 * D_ATTN:2 * D_ATTN + N_HEADS] + bf_ref[...]
        log_f = jnp.minimum(z, 0.0) - jnp.log1p(jnp.exp(-jnp.abs(z)))

        c0 = D_ATTN
        cu = (_dot(xn, wtok_ref[:, c0 + D_CONV:c0 + 2 * D_CONV])
              * _dot(xn, wtok_ref[:, c0 + 2 * D_CONV:c0 + 3 * D_CONV]))
        cu_scr[SUBLANES + r0:SUBLANES + r0 + sub, :] = cu

        cum = _split_dot(log_f, tri_ref[...]) + carry_scr[:, 0:1]
        carry_scr[...] = jnp.broadcast_to(cum[:, sub - 1:sub], carry_scr.shape)
        bias = -LOG2E * cum
        if meta:
            cum_ref[...] = cum
            lane = lax.broadcasted_iota(jnp.int32, cum.shape, 1)
            bias = jnp.where(lane < N_META, bias, MASKED)
            tail_ref[...] = cu_scr[N_META:N_META + SUBLANES, :]
        else:
            cu1 = cu_scr[SUBLANES + r0 - 1:SUBLANES + r0 - 1 + sub, :]
            cu2 = cu_scr[SUBLANES + r0 - 2:SUBLANES + r0 - 2 + sub, :]
            conv = convw_ref[0:1, :] * cu2 + convw_ref[1:2, :] * cu1 + convw_ref[2:3, :] * cu
            yc = _dot(xn, wtok_ref[:, c0:c0 + D_CONV]) * conv
        bias_cols = _dot(perm_ref[...], _stack_terms(bias)).T

        low = lax.broadcasted_iota(jnp.int32, (1, PAIR), 1) < HEAD_DIM

        def keys(pairs):
            k = _dot(xn, wtok_ref[:, pairs[0] * PAIR:(pairs[-1] + 1) * PAIR])
            for h in range(2 * pairs[0], 2 * pairs[-1] + 2):
                kp = k[:, (h // 2 - pairs[0]) * PAIR:(h // 2 - pairs[0] + 1) * PAIR]
                ka_ref[0, h, r0:r0 + sub] = jnp.where(low == (h % 2 == 0), kp, bias_cols).astype(BF16)

        keys((0, 1))
        if not meta:
            ms = _dot((yc * yc).astype(BF16), gmat_ref[...])
            yc_ref[0, r0:r0 + sub] = (yc * lax.rsqrt(ms + EPS) * gconv_ref[...]).astype(BF16)
        keys((2, 3))

    for r0 in range(0, tile, sub):
        rows_at(r0)
    if not meta:
        cu_scr[0:SUBLANES, :] = cu_scr[tile:tile + SUBLANES, :]


def _projection(x, gmix, wtok, wrow, bf_col, convw, gconv, gmat, cum0, tail0, *, tile, sub, meta):
    nb, seq, _ = x.shape
    nt = seq // tile
    tri = jnp.triu(jnp.ones((sub, sub), F32)).astype(BF16)
    out_lane = jnp.arange(PAIR) % HEAD_DIM
    src_row = jnp.where(out_lane < BIAS_TERMS * N_HEADS, out_lane, -1)
    perm = (src_row[:, None] == jnp.arange(4 * N_HEADS)[None, :]).astype(BF16)
    const = lambda shape: pl.BlockSpec(shape, lambda b, t: (0,) * len(shape))
    in_specs = [
        pl.BlockSpec((1, tile, D_MODEL), lambda b, t: (b, t, 0)),
        const(gmix.shape), const(wtok.shape), const(wrow.shape), const(bf_col.shape), const(convw.shape),
        const(gconv.shape), const(gmat.shape), const(tri.shape), const(perm.shape), const(cum0.shape),
        const(tail0.shape),
    ]
    tok_spec = pl.BlockSpec((1, tile, D_CONV), lambda b, t: (b, t, 0))
    row_spec = pl.BlockSpec((1, D_ATTN, tile), lambda b, t: (b, 0, t))
    ka_spec = pl.BlockSpec((1, N_HEADS, tile, PAIR), lambda b, t: (b, 0, t, 0))
    tok_shape = jax.ShapeDtypeStruct((nb, seq, D_CONV), BF16)
    row_shape = jax.ShapeDtypeStruct((nb, D_ATTN, seq), BF16)
    ka_shape = jax.ShapeDtypeStruct((nb, N_HEADS, seq, PAIR), BF16)
    if meta:
        out_specs = [ka_spec, row_spec, const((N_HEADS, tile)), const((SUBLANES, D_CONV))]
        out_shape = [ka_shape, row_shape, jax.ShapeDtypeStruct((N_HEADS, tile), F32),
                     jax.ShapeDtypeStruct((SUBLANES, D_CONV), F32)]
    else:
        out_specs = [row_spec, ka_spec, row_spec, tok_spec]
        out_shape = [row_shape, ka_shape, row_shape, tok_shape]
    return pl.pallas_call(
        functools.partial(_proj_kernel, tile=tile, sub=sub, meta=meta),
        grid=(nb, nt),
        in_specs=in_specs,
        out_specs=out_specs,
        out_shape=out_shape,
        scratch_shapes=[pltpu.VMEM((tile + SUBLANES, D_CONV), F32), pltpu.VMEM((N_HEADS, LANES), F32)],
        compiler_params=pltpu.CompilerParams(
            dimension_semantics=("arbitrary", "arbitrary"), vmem_limit_bytes=48 * 1024 * 1024),
        name="proj_meta" if meta else "proj",
    )(x, gmix, wtok, wrow, bf_col, convw, gconv, gmat, tri, perm, cum0, tail0)


def _attn_kernel(qt_ref, ka_ref, vt_ref, kam_ref, vtm_ref, g_ref, o_ref,
                 m_scr, a_scr, acc_scr, s_scr, p_scr, sm_scr, pm_scr):
    tq, tk = ATTN_TQ, ATTN_TK
    assert tq == 2 * tk, "the pipeline fill assumes two diagonal key tiles per query block"
    n_blocks = qt_ref.shape[2] // tq
    heads = range(N_HEADS)
    real_vt = vt_ref.at[0]
    sub = lax.broadcasted_iota(jnp.int32, (HEAD_DIM, tq), 0)
    ones = jnp.ones((BF16_ROWS, tk), BF16)

    def query(h, q0):
        qh = qt_ref[0, h * HEAD_DIM:(h + 1) * HEAD_DIM, q0:q0 + tq]
        ind = ((sub % N_HEADS == h) & (sub < BIAS_TERMS * N_HEADS)).astype(BF16)
        return jnp.concatenate([qh, ind] if h % 2 == 0 else [ind, qh], axis=0)

    def values(vt, h, k0, width):
        return jnp.concatenate([vt[h * HEAD_DIM:(h + 1) * HEAD_DIM, pl.ds(k0, width)], ones[:, 0:width]], axis=0)

    class Block:
        def __init__(self, i):
            self.q0, self.n_full = i * tq, i * (tq // tk)
            self.qs = [query(h, self.q0) for h in heads]
            w = i % 2
            self.s, self.p = s_scr.at[w], p_scr.at[w]
            self.m, self.a, self.acc = m_scr.at[w], a_scr.at[w], acc_scr.at[w]

        def tile_start(self, j):
            if isinstance(j, int):
                return self.q0 + j * tk if j < 2 else (j - 2) * tk
            return pl.multiple_of(jnp.where(j < 2, self.q0 + j * tk, (j - 2) * tk), tk)

    def scores(b, h, keys, s_buf, width, diag_offset=None):
        lo = 0 if diag_offset is None else diag_offset
        s = _dot(keys, b.qs[h][:, lo:tq])
        if diag_offset is not None:
            key = lax.broadcasted_iota(jnp.int32, s.shape, 0)
            s = jnp.where(key <= lax.broadcasted_iota(jnp.int32, s.shape, 1), s, MASKED)
            if lo:
                s_buf[h, 0:width, 0:lo] = jnp.full((width, lo), MASKED, F32)
        s_buf[h, 0:width, lo:tq] = s

    def softmax(b, h, s_buf, p_buf, width, first=False):
        m_new = jnp.max(s_buf[h, 0:width], axis=0, keepdims=True)
        if not first:
            m = b.m[h]
            m_new = jnp.maximum(m, m_new)
            b.a[h] = jnp.exp2(m - m_new)
        b.m[h] = m_new
        p_buf[h, 0:width] = jnp.exp2(s_buf[h, 0:width] - m_new).astype(BF16)

    def weighted(b, h, vt, p_buf, k0, width, first=False):
        pv = _dot(values(vt, h, k0, width), p_buf[h, 0:width])
        b.acc[h] = pv if first else b.a[h] * b.acc[h] + pv

    def real_keys(h, k0):
        return ka_ref[0, h, pl.ds(k0, tk), :]

    def fill_scores(b, h):
        scores(b, h, kam_ref[h, 0:N_META, :], sm_scr, N_META)
        scores(b, h, real_keys(h, b.q0), b.s, tk, diag_offset=0)

    def fill_meta(b, h):
        softmax(b, h, sm_scr, pm_scr, N_META, first=True)

    def fill_rest(b, h):
        weighted(b, h, vtm_ref, pm_scr, 0, N_META, first=True)
        softmax(b, h, b.s, b.p, tk)
        scores(b, h, real_keys(h, b.q0 + tk), b.s, tk, diag_offset=tk)

    def trip(b, c):
        k_pv, k_qk = b.tile_start(c), pl.multiple_of(c * tk, tk)
        for h in heads:
            weighted(b, h, real_vt, b.p, k_pv, tk)
            softmax(b, h, b.s, b.p, tk)
            scores(b, h, real_keys(h, k_qk), b.s, tk)

    def drain_softmax(b, h):
        weighted(b, h, real_vt, b.p, b.tile_start(b.n_full), tk)
        softmax(b, h, b.s, b.p, tk)

    def drain_values(b, h):
        weighted(b, h, real_vt, b.p, b.tile_start(b.n_full + 1), tk)

    def finish(b):
        def normed(h):
            acc = b.acc[h]
            o = acc[0:HEAD_DIM] / acc[HEAD_DIM:HEAD_DIM + 1]
            return o * lax.rsqrt(jnp.mean(o * o, axis=0, keepdims=True) + EPS)

        for j in range(N_PAIRS):
            y = jnp.concatenate([normed(2 * j), normed(2 * j + 1)], axis=0).T
            o_ref[0, b.q0:b.q0 + tq, j * PAIR:(j + 1) * PAIR] = (
                y * g_ref[:, j * PAIR:(j + 1) * PAIR]).astype(BF16)

    blk = Block(0)
    for h in heads:
        fill_scores(blk, h)
    for h in heads:
        fill_meta(blk, h)
    for h in heads:
        fill_rest(blk, h)
    for i in range(n_blocks):

        def body(c, carry, b=blk):
            for u in range(tq // tk):
                trip(b, (tq // tk) * c + u)
            return carry

        lax.fori_loop(0, jnp.minimum(pl.program_id(0) + i, i), body, 0)
        nxt = Block(i + 1) if i + 1 < n_blocks else None
        for h in heads:
            drain_softmax(blk, h)
            if nxt:
                fill_scores(nxt, h)
        if nxt:
            for h in heads:
                fill_meta(nxt, h)
        for h in heads:
            drain_values(blk, h)
            if nxt:
                fill_rest(nxt, h)
        finish(blk)
        blk = nxt


def _attention(qt, ka, vt, kam, vtm, gattn):
    nb, _, seq = qt.shape
    tq, tk = ATTN_TQ, ATTN_TK
    whole = lambda a: pl.BlockSpec(a.shape, lambda b: (0,) * a.ndim)
    per_batch = lambda a: pl.BlockSpec((1,) + a.shape[1:], lambda b: (b,) + (0,) * (a.ndim - 1))
    return pl.pallas_call(
        _attn_kernel,
        grid=(nb,),
        in_specs=[per_batch(qt), per_batch(ka), per_batch(vt), whole(kam), whole(vtm), whole(gattn)],
        out_specs=pl.BlockSpec((1, seq, D_ATTN), lambda b: (b, 0, 0)),
        out_shape=jax.ShapeDtypeStruct((nb, seq, D_ATTN), BF16),
        scratch_shapes=[
            pltpu.VMEM((2, N_HEADS, 1, tq), F32), pltpu.VMEM((2, N_HEADS, 1, tq), F32),
            pltpu.VMEM((2, N_HEADS, HEAD_DIM + BF16_ROWS, tq), F32),
            pltpu.VMEM((2, N_HEADS, tk, tq), F32), pltpu.VMEM((2, N_HEADS, tk, tq), BF16),
            pltpu.VMEM((N_HEADS, N_META, tq), F32), pltpu.VMEM((N_HEADS, N_META, tq), BF16),
        ],
        compiler_params=pltpu.CompilerParams(
            dimension_semantics=("arbitrary",), vmem_limit_bytes=56 * 1024 * 1024),
        name="attn",
    )(qt, ka, vt, kam, vtm, gattn)


def _mlp_kernel(x_ref, ya_ref, yc_ref, wo_ref, gmlp_ref, w1_ref, w2_ref, gfin_ref, o_ref):
    h = (x_ref[0] + _dot(ya_ref[0], wo_ref[0:D_ATTN, :]) + _dot(yc_ref[0], wo_ref[D_ATTN:D_ATTN + D_CONV, :]))
    hn = _rms_norm(h, gmlp_ref[...]).astype(BF16)
    ff = None
    for c in range(0, D_FF, FF_CHUNK):
        a = jnp.square(jnp.maximum(_dot(hn, w1_ref[:, c:c + FF_CHUNK]), 0.0)).astype(BF16)
        part = _dot(a, w2_ref[c:c + FF_CHUNK, :])
        ff = part if ff is None else ff + part
    o_ref[0] = _rms_norm(h + ff, gfin_ref[...])


def _mlp(x, ya, yc, wo, gmlp, w1, w2, gfin):
    nb, seq, _ = x.shape
    t = MLP_TILE
    const = lambda shape: pl.BlockSpec(shape, lambda b, i: (0,) * len(shape), pipeline_mode=pl.Buffered(1))
    return pl.pallas_call(
        _mlp_kernel,
        grid=(nb, seq // t),
        in_specs=[
            pl.BlockSpec((1, t, D_MODEL), lambda b, i: (b, i, 0)),
            pl.BlockSpec((1, t, D_ATTN), lambda b, i: (b, i, 0)),
            pl.BlockSpec((1, t, D_CONV), lambda b, i: (b, i, 0)),
            const(wo.shape), const(gmlp.shape), const(w1.shape), const(w2.shape), const(gfin.shape),
        ],
        out_specs=pl.BlockSpec((1, t, D_MODEL), lambda b, i: (b, i, 0)),
        out_shape=jax.ShapeDtypeStruct((nb, seq, D_MODEL), F32),
        compiler_params=pltpu.CompilerParams(
            dimension_semantics=("parallel", "parallel"), vmem_limit_bytes=56 * 1024 * 1024),
        name="mlp",
    )(x, ya, yc, wo, gmlp, w1, w2, gfin)


def kernel(x, meta_tokens, norm_mix_g, w_in, b_f, conv_w, out_norm_g, w_out, norm_mlp_g, w_ff1, w_ff2,
           final_norm_g):
    assert w_in.shape[0] == 1, "single-layer block"
    w = w_in[0]
    o_q, o_k, o_v, o_f = 0, D_ATTN, 2 * D_ATTN, 3 * D_ATTN
    o_b = o_f + N_HEADS
    wq = w[:, o_q:o_k] * (LOG2E * HEAD_DIM ** -0.5)
    wtok = jnp.concatenate([w[:, o_k:o_v], w[:, o_b:]], axis=1).astype(BF16)
    wrow = jnp.concatenate([wq, w[:, o_v:o_f], w[:, o_f:o_b], jnp.zeros((D_MODEL, SUBLANES), F32)], axis=1)
    wrow = wrow.T.astype(BF16)
    gmix = norm_mix_g[0][None, :]
    bf_col = b_f[0][:, None]
    convw = jnp.concatenate([conv_w[0], jnp.zeros((SUBLANES - CONV_WIDTH, D_CONV), F32)], axis=0)
    gattn = out_norm_g[0, :D_ATTN][None, :]
    gconv = out_norm_g[0, D_ATTN:][None, :]
    group = jnp.arange(D_CONV) // HEAD_DIM
    gmat = ((group[:, None] == group[None, :]).astype(F32) / HEAD_DIM).astype(BF16)

    meta = jnp.zeros((1, META_PAD, D_MODEL), F32).at[0, :N_META].set(meta_tokens)
    proj = functools.partial(_projection, gmix=gmix, wtok=wtok, wrow=wrow, bf_col=bf_col, convw=convw,
                             gconv=gconv, gmat=gmat)
    kam, vtm, cumm, tailm = proj(meta, cum0=jnp.zeros((N_HEADS, LANES), F32),
                                 tail0=jnp.zeros((SUBLANES, D_CONV), F32), tile=META_PAD, sub=META_PAD,
                                 meta=True)
    qt, ka, vt, yc = proj(x, cum0=cumm, tail0=tailm, tile=PROJ_TILE, sub=PROJ_SUB, meta=False)

    ya = _attention(qt, ka, vt, kam[0], vtm[0], gattn)

    return _mlp(x, ya, yc, w_out[0].astype(BF16), norm_mlp_g[0][None, :], w_ff1[0].astype(BF16),
                w_ff2[0].astype(BF16), final_norm_g[None, :])
```

```python
import functools

import jax
import jax.numpy as jnp
from jax import lax
from jax.experimental import pallas as pl
from jax.experimental.pallas import tpu as pltpu

D_MODEL = 1024
N_META = 16
HEAD_DIM = 64
D_ATTN = 512
N_HEADS = D_ATTN // HEAD_DIM
D_CONV = 512
CONV_WIDTH = 3
D_FF = 4096
EPS = 1e-5
MASKED = -1e30
LOG2E = 1.4426950408889634
BIAS_TERMS = 3

LANES = 128
SUBLANES = 8
BF16_ROWS = 16
PAIR = 2 * HEAD_DIM
N_PAIRS = N_HEADS // 2
META_PAD = LANES

PROJ_TILE = 1024
PROJ_SUB = 512
ATTN_TQ = 512
ATTN_TK = 256
MLP_TILE = 512
FF_CHUNK = 1024

BF16 = jnp.bfloat16
F32 = jnp.float32


def _dot(a, b):
    return jnp.dot(a, b, preferred_element_type=F32)


def _dot_nt(a, b):
    return lax.dot_general(a, b, (((1,), (1,)), ((), ())), preferred_element_type=F32)


def _rms_norm(x, g):
    ms = jnp.mean(x * x, axis=-1, keepdims=True)
    return x * lax.rsqrt(ms + EPS) * g


def _split_bf16(x, parts):
    out = []
    r = x
    for _ in range(parts):
        t = r.astype(BF16)
        out.append(t)
        r = r - t.astype(F32)
    return out


def _stack_terms(x):
    terms = [t.astype(F32) for t in _split_bf16(x, BIAS_TERMS)]
    return jnp.concatenate(terms + [jnp.zeros_like(x)], axis=0).astype(BF16)


def _split_dot(x, w):
    sums = _dot(_stack_terms(x), w)
    h = x.shape[0]
    return sums[0:h] + sums[h:2 * h] + sums[2 * h:3 * h]


def _proj_kernel(x_ref, gmix_ref, wtok_ref, wrow_ref, bf_ref, convw_ref, gconv_ref, tri_ref, perm_ref,
                 cum0_ref, tail0_ref, *refs, tile, sub, meta):
    if meta:
        ka_ref, vt_ref, cum_ref, tail_ref, cu_scr, carry_scr = refs
    else:
        qt_ref, ka_ref, vt_ref, yc_ref, cu_scr, carry_scr = refs

    @pl.when(pl.program_id(1) == 0)
    def _():
        carry_scr[...] = jnp.broadcast_to(cum0_ref[:, N_META - 1:N_META], carry_scr.shape)
        cu_scr[0:SUBLANES, :] = tail0_ref[...]

    def rows_at(r0):
        xn = _rms_norm(x_ref[0, r0:r0 + sub], gmix_ref[...]).astype(BF16)

        rows = _dot_nt(wrow_ref[...], xn)
        if not meta:
            qt_ref[0, :, r0:r0 + sub] = rows[0:D_ATTN].astype(BF16)
        vt_ref[0, :, r0:r0 + sub] = rows[D_ATTN:2 * D_ATTN].astype(BF16)
        z = rows[2 * D_ATTN:2 * D_ATTN + N_HEADS] + bf_ref[...]
        log_f = jnp.minimum(z, 0.0) - jnp.log1p(jnp.exp(-jnp.abs(z)))

        c0 = D_ATTN
        cu = (_dot(xn, wtok_ref[:, c0 + D_CONV:c0 + 2 * D_CONV])
              * _dot(xn, wtok_ref[:, c0 + 2 * D_CONV:c0 + 3 * D_CONV]))
        cu_scr[SUBLANES + r0:SUBLANES + r0 + sub, :] = cu

        cum = _split_dot(log_f, tri_ref[...]) + carry_scr[:, 0:1]
        carry_scr[...] = jnp.broadcast_to(cum[:, sub - 1:sub], carry_scr.shape)
        bias = -LOG2E * cum
        if meta:
            cum_ref[...] = cum
            lane = lax.broadcasted_iota(jnp.int32, cum.shape, 1)
            bias = jnp.where(lane < N_META, bias, MASKED)
            tail_ref[...] = cu_scr[N_META:N_META + SUBLANES, :]
        else:
            cu1 = cu_scr[SUBLANES + r0 - 1:SUBLANES + r0 - 1 + sub, :]
            cu2 = cu_scr[SUBLANES + r0 - 2:SUBLANES + r0 - 2 + sub, :]
            conv = convw_ref[0:1, :] * cu2 + convw_ref[1:2, :] * cu1 + convw_ref[2:3, :] * cu
            yc = _dot(xn, wtok_ref[:, c0:c0 + D_CONV]) * conv
        k_lo = _dot(xn, wtok_ref[:, 0:2 * PAIR])
        bias_cols = _dot(perm_ref[...], _stack_terms(bias)).T
        k_hi = _dot(xn, wtok_ref[:, 2 * PAIR:4 * PAIR])
        low = lax.broadcasted_iota(jnp.int32, (1, PAIR), 1) < HEAD_DIM
        if not meta:
            yt = yc.T.reshape(D_CONV // HEAD_DIM, HEAD_DIM, sub)
            ms = jnp.mean(yt * yt, axis=1, keepdims=True)
            yn = (yt * lax.rsqrt(ms + EPS)).reshape(D_CONV, sub).T
            yc_ref[0, r0:r0 + sub] = (yn * gconv_ref[...]).astype(BF16)
        for h in range(N_HEADS):
            kp = (k_lo, k_hi)[h // 4][:, (h // 2 % 2) * PAIR:(h // 2 % 2 + 1) * PAIR]
            ka_ref[0, h, r0:r0 + sub] = jnp.where(low == (h % 2 == 0), kp, bias_cols).astype(BF16)

    for r0 in range(0, tile, sub):
        rows_at(r0)
    if not meta:
        cu_scr[0:SUBLANES, :] = cu_scr[tile:tile + SUBLANES, :]


def _projection(x, gmix, wtok, wrow, bf_col, convw, gconv, cum0, tail0, *, tile, sub, meta):
    nb, seq, _ = x.shape
    nt = seq // tile
    tri = jnp.triu(jnp.ones((sub, sub), F32)).astype(BF16)
    out_lane = jnp.arange(PAIR) % HEAD_DIM
    src_row = jnp.where(out_lane < BIAS_TERMS * N_HEADS, out_lane, -1)
    perm = (src_row[:, None] == jnp.arange(4 * N_HEADS)[None, :]).astype(BF16)
    const = lambda shape: pl.BlockSpec(shape, lambda b, t: (0,) * len(shape))
    in_specs = [
        pl.BlockSpec((1, tile, D_MODEL), lambda b, t: (b, t, 0)),
        const(gmix.shape), const(wtok.shape), const(wrow.shape), const(bf_col.shape), const(convw.shape),
        const(gconv.shape), const(tri.shape), const(perm.shape), const(cum0.shape),
        const(tail0.shape),
    ]
    tok_spec = pl.BlockSpec((1, tile, D_CONV), lambda b, t: (b, t, 0))
    row_spec = pl.BlockSpec((1, D_ATTN, tile), lambda b, t: (b, 0, t))
    ka_spec = pl.BlockSpec((1, N_HEADS, tile, PAIR), lambda b, t: (b, 0, t, 0))
    tok_shape = jax.ShapeDtypeStruct((nb, seq, D_CONV), BF16)
    row_shape = jax.ShapeDtypeStruct((nb, D_ATTN, seq), BF16)
    ka_shape = jax.ShapeDtypeStruct((nb, N_HEADS, seq, PAIR), BF16)
    if meta:
        out_specs = [ka_spec, row_spec, const((N_HEADS, tile)), const((SUBLANES, D_CONV))]
        out_shape = [ka_shape, row_shape, jax.ShapeDtypeStruct((N_HEADS, tile), F32),
                     jax.ShapeDtypeStruct((SUBLANES, D_CONV), F32)]
    else:
        out_specs = [row_spec, ka_spec, row_spec, tok_spec]
        out_shape = [row_shape, ka_shape, row_shape, tok_shape]
    return pl.pallas_call(
        functools.partial(_proj_kernel, tile=tile, sub=sub, meta=meta),
        grid=(nb, nt),
        in_specs=in_specs,
        out_specs=out_specs,
        out_shape=out_shape,
        scratch_shapes=[pltpu.VMEM((tile + SUBLANES, D_CONV), F32), pltpu.VMEM((N_HEADS, LANES), F32)],
        compiler_params=pltpu.CompilerParams(
            dimension_semantics=("arbitrary", "arbitrary"), vmem_limit_bytes=48 * 1024 * 1024),
        name="proj_meta" if meta else "proj",
    )(x, gmix, wtok, wrow, bf_col, convw, gconv, tri, perm, cum0, tail0)


def _attn_kernel(qt_ref, ka_ref, vt_ref, kam_ref, vtm_ref, g_ref, o_ref,
                 m_scr, a_scr, acc_scr, s_scr, p_scr, sm_scr, pm_scr):
    tq, tk = ATTN_TQ, ATTN_TK
    assert tq == 2 * tk, "the pipeline fill assumes two diagonal key tiles per query block"
    n_blocks = qt_ref.shape[2] // tq
    heads = range(N_HEADS)
    real_vt = vt_ref.at[0]
    sub = lax.broadcasted_iota(jnp.int32, (HEAD_DIM, tq), 0)
    ones = jnp.ones((BF16_ROWS, tk), BF16)

    def query(h, q0):
        qh = qt_ref[0, h * HEAD_DIM:(h + 1) * HEAD_DIM, q0:q0 + tq]
        ind = ((sub % N_HEADS == h) & (sub < BIAS_TERMS * N_HEADS)).astype(BF16)
        return jnp.concatenate([qh, ind] if h % 2 == 0 else [ind, qh], axis=0)

    def values(vt, h, k0, width):
        return jnp.concatenate([vt[h * HEAD_DIM:(h + 1) * HEAD_DIM, pl.ds(k0, width)], ones[:, 0:width]], axis=0)

    class Block:
        def __init__(self, i):
            self.q0, self.n_full = i * tq, i * (tq // tk)
            self.qs = [query(h, self.q0) for h in heads]
            w = i % 2
            self.s, self.p = s_scr.at[w], p_scr.at[w]
            self.m, self.a, self.acc = m_scr.at[w], a_scr.at[w], acc_scr.at[w]

        def tile_start(self, j):
            if isinstance(j, int):
                return self.q0 + j * tk if j < 2 else (j - 2) * tk
            return pl.multiple_of(jnp.where(j < 2, self.q0 + j * tk, (j - 2) * tk), tk)

    def scores(b, h, keys, s_buf, width, diag_offset=None):
        lo = 0 if diag_offset is None else diag_offset
        s = _dot(keys, b.qs[h][:, lo:tq])
        if diag_offset is not None:
            key = lax.broadcasted_iota(jnp.int32, s.shape, 0)
            s = jnp.where(key <= lax.broadcasted_iota(jnp.int32, s.shape, 1), s, MASKED)
            if lo:
                s_buf[h, 0:width, 0:lo] = jnp.full((width, lo), MASKED, F32)
        s_buf[h, 0:width, lo:tq] = s

    def softmax(b, h, s_buf, p_buf, width, first=False):
        m_new = jnp.max(s_buf[h, 0:width], axis=0, keepdims=True)
        if not first:
            m = b.m[h]
            m_new = jnp.maximum(m, m_new)
            b.a[h] = jnp.exp2(m - m_new)
        b.m[h] = m_new
        p_buf[h, 0:width] = jnp.exp2(s_buf[h, 0:width] - m_new).astype(BF16)

    def weighted(b, h, vt, p_buf, k0, width, first=False):
        pv = _dot(values(vt, h, k0, width), p_buf[h, 0:width])
        b.acc[h] = pv if first else b.a[h] * b.acc[h] + pv

    def real_keys(h, k0):
        return ka_ref[0, h, pl.ds(k0, tk), :]

    def fill_scores(b, h):
        scores(b, h, kam_ref[h, 0:N_META, :], sm_scr, N_META)
        scores(b, h, real_keys(h, b.q0), b.s, tk, diag_offset=0)

    def fill_meta(b, h):
        softmax(b, h, sm_scr, pm_scr, N_META, first=True)

    def fill_rest(b, h):
        weighted(b, h, vtm_ref, pm_scr, 0, N_META, first=True)
        softmax(b, h, b.s, b.p, tk)
        scores(b, h, real_keys(h, b.q0 + tk), b.s, tk, diag_offset=tk)

    def trip(b, c):
        k_pv, k_qk = b.tile_start(c), pl.multiple_of(c * tk, tk)
        for h in heads:
            weighted(b, h, real_vt, b.p, k_pv, tk)
            softmax(b, h, b.s, b.p, tk)
            scores(b, h, real_keys(h, k_qk), b.s, tk)

    def drain_softmax(b, h):
        weighted(b, h, real_vt, b.p, b.tile_start(b.n_full), tk)
        softmax(b, h, b.s, b.p, tk)

    def drain_values(b, h):
        weighted(b, h, real_vt, b.p, b.tile_start(b.n_full + 1), tk)

    def finish(b):
        def normed(h):
            acc = b.acc[h]
            o = acc[0:HEAD_DIM] / acc[HEAD_DIM:HEAD_DIM + 1]
            return o * lax.rsqrt(jnp.mean(o * o, axis=0, keepdims=True) + EPS)

        for j in range(N_PAIRS):
            y = jnp.concatenate([normed(2 * j), normed(2 * j + 1)], axis=0).T
            o_ref[0, b.q0:b.q0 + tq, j * PAIR:(j + 1) * PAIR] = (
                y * g_ref[:, j * PAIR:(j + 1) * PAIR]).astype(BF16)

    blk = Block(0)
    for h in heads:
        fill_scores(blk, h)
    for h in heads:
        fill_meta(blk, h)
    for h in heads:
        fill_rest(blk, h)
    for i in range(n_blocks):

        def body(c, carry, b=blk):
            for u in range(tq // tk):
                trip(b, (tq // tk) * c + u)
            return carry

        lax.fori_loop(0, jnp.minimum(pl.program_id(0) + i, i), body, 0)
        nxt = Block(i + 1) if i + 1 < n_blocks else None
        for h in heads:
            drain_softmax(blk, h)
            if nxt:
                fill_scores(nxt, h)
        if nxt:
            for h in heads:
                fill_meta(nxt, h)
        for h in heads:
            drain_values(blk, h)
            if nxt:
                fill_rest(nxt, h)
        finish(blk)
        blk = nxt


def _attention(qt, ka, vt, kam, vtm, gattn):
    nb, _, seq = qt.shape
    tq, tk = ATTN_TQ, ATTN_TK
    whole = lambda a: pl.BlockSpec(a.shape, lambda b: (0,) * a.ndim)
    per_batch = lambda a: pl.BlockSpec((1,) + a.shape[1:], lambda b: (b,) + (0,) * (a.ndim - 1))
    return pl.pallas_call(
        _attn_kernel,
        grid=(nb,),
        in_specs=[per_batch(qt), per_batch(ka), per_batch(vt), whole(kam), whole(vtm), whole(gattn)],
        out_specs=pl.BlockSpec((1, seq, D_ATTN), lambda b: (b, 0, 0)),
        out_shape=jax.ShapeDtypeStruct((nb, seq, D_ATTN), BF16),
        scratch_shapes=[
            pltpu.VMEM((2, N_HEADS, 1, tq), F32), pltpu.VMEM((2, N_HEADS, 1, tq), F32),
            pltpu.VMEM((2, N_HEADS, HEAD_DIM + BF16_ROWS, tq), F32),
            pltpu.VMEM((2, N_HEADS, tk, tq), F32), pltpu.VMEM((2, N_HEADS, tk, tq), BF16),
            pltpu.VMEM((N_HEADS, N_META, tq), F32), pltpu.VMEM((N_HEADS, N_META, tq), BF16),
        ],
        compiler_params=pltpu.CompilerParams(
            dimension_semantics=("arbitrary",), vmem_limit_bytes=56 * 1024 * 1024),
        name="attn",
    )(qt, ka, vt, kam, vtm, gattn)


def _mlp_kernel(x_ref, ya_ref, yc_ref, wo_ref, gmlp_ref, w1_ref, w2_ref, gfin_ref, o_ref):
    h = (x_ref[0] + _dot(ya_ref[0], wo_ref[0:D_ATTN, :]) + _dot(yc_ref[0], wo_ref[D_ATTN:D_ATTN + D_CONV, :]))
    hn = _rms_norm(h, gmlp_ref[...]).astype(BF16)
    ff = None
    for c in range(0, D_FF, FF_CHUNK):
        a = jnp.square(jnp.maximum(_dot(hn, w1_ref[:, c:c + FF_CHUNK]), 0.0)).astype(BF16)
        part = _dot(a, w2_ref[c:c + FF_CHUNK, :])
        ff = part if ff is None else ff + part
    o_ref[0] = _rms_norm(h + ff, gfin_ref[...])


def _mlp(x, ya, yc, wo, gmlp, w1, w2, gfin):
    nb, seq, _ = x.shape
    t = MLP_TILE
    const = lambda shape: pl.BlockSpec(shape, lambda b, i: (0,) * len(shape), pipeline_mode=pl.Buffered(1))
    return pl.pallas_call(
        _mlp_kernel,
        grid=(nb, seq // t),
        in_specs=[
            pl.BlockSpec((1, t, D_MODEL), lambda b, i: (b, i, 0)),
            pl.BlockSpec((1, t, D_ATTN), lambda b, i: (b, i, 0)),
            pl.BlockSpec((1, t, D_CONV), lambda b, i: (b, i, 0)),
            const(wo.shape), const(gmlp.shape), const(w1.shape), const(w2.shape), const(gfin.shape),
        ],
        out_specs=pl.BlockSpec((1, t, D_MODEL), lambda b, i: (b, i, 0)),
        out_shape=jax.ShapeDtypeStruct((nb, seq, D_MODEL), F32),
        compiler_params=pltpu.CompilerParams(
            dimension_semantics=("parallel", "parallel"), vmem_limit_bytes=56 * 1024 * 1024),
        name="mlp",
    )(x, ya, yc, wo, gmlp, w1, w2, gfin)


def kernel(x, meta_tokens, norm_mix_g, w_in, b_f, conv_w, out_norm_g, w_out, norm_mlp_g, w_ff1, w_ff2,
           final_norm_g):
    assert w_in.shape[0] == 1, "single-layer block"
    w = w_in[0]
    o_q, o_k, o_v, o_f = 0, D_ATTN, 2 * D_ATTN, 3 * D_ATTN
    o_b = o_f + N_HEADS
    wq = w[:, o_q:o_k] * (LOG2E * HEAD_DIM ** -0.5)
    wtok = jnp.concatenate([w[:, o_k:o_v], w[:, o_b:]], axis=1).astype(BF16)
    wrow = jnp.concatenate([wq, w[:, o_v:o_f], w[:, o_f:o_b], jnp.zeros((D_MODEL, SUBLANES), F32)], axis=1)
    wrow = wrow.T.astype(BF16)
    gmix = norm_mix_g[0][None, :]
    bf_col = b_f[0][:, None]
    convw = jnp.concatenate([conv_w[0], jnp.zeros((SUBLANES - CONV_WIDTH, D_CONV), F32)], axis=0)
    gattn = out_norm_g[0, :D_ATTN][None, :]
    gconv = out_norm_g[0, D_ATTN:][None, :]

    meta = jnp.zeros((1, META_PAD, D_MODEL), F32).at[0, :N_META].set(meta_tokens)
    proj = functools.partial(_projection, gmix=gmix, wtok=wtok, wrow=wrow, bf_col=bf_col, convw=convw,
                             gconv=gconv)
    kam, vtm, cumm, tailm = proj(meta, cum0=jnp.zeros((N_HEADS, LANES), F32),
                                 tail0=jnp.zeros((SUBLANES, D_CONV), F32), tile=META_PAD, sub=META_PAD,
                                 meta=True)
    qt, ka, vt, yc = proj(x, cum0=cumm, tail0=tailm, tile=PROJ_TILE, sub=PROJ_SUB, meta=False)

    ya = _attention(qt, ka, vt, kam[0], vtm[0], gattn)

    return _mlp(x, ya, yc, w_out[0].astype(BF16), norm_mlp_g[0][None, :], w_ff1[0].astype(BF16),
                w_ff2[0].astype(BF16), final_norm_g[None, :])
```

```python
import functools

import jax
import jax.numpy as jnp
from jax import lax
from jax.experimental import pallas as pl
from jax.experimental.pallas import tpu as pltpu

D_MODEL = 1024
N_META = 16
HEAD_DIM = 64
D_ATTN = 512
N_HEADS = D_ATTN // HEAD_DIM
D_CONV = 512
CONV_WIDTH = 3
D_FF = 4096
EPS = 1e-5
MASKED = -1e30
LOG2E = 1.4426950408889634
BIAS_TERMS = 3

LANES = 128
SUBLANES = 8
BF16_ROWS = 16
PAIR = 2 * HEAD_DIM
N_PAIRS = N_HEADS // 2
META_PAD = LANES

PROJ_TILE = 1024
PROJ_SUB = 512
ATTN_TQ = 512
ATTN_TK = 256
MLP_TILE = 512
FF_CHUNK = 1024

BF16 = jnp.bfloat16
F32 = jnp.float32


def _dot(a, b):
    return jnp.dot(a, b, preferred_element_type=F32)


def _dot_nt(a, b):
    return lax.dot_general(a, b, (((1,), (1,)), ((), ())), preferred_element_type=F32)


def _rms_norm(x, g):
    ms = jnp.mean(x * x, axis=-1, keepdims=True)
    return x * lax.rsqrt(ms + EPS) * g


def _split_bf16(x, parts):
    out = []
    r = x
    for _ in range(parts):
        t = r.astype(BF16)
        out.append(t)
        r = r - t.astype(F32)
    return out


def _stack_terms(x):
    terms = [t.astype(F32) for t in _split_bf16(x, BIAS_TERMS)]
    return jnp.concatenate(terms + [jnp.zeros_like(x)], axis=0).astype(BF16)


def _split_dot(x, w):
    sums = _dot(_stack_terms(x), w)
    h = x.shape[0]
    return sums[0:h] + sums[h:2 * h] + sums[2 * h:3 * h]


def _proj_kernel(x_ref, gmix_ref, wtok_ref, wrow_ref, bf_ref, convw_ref, gconv_ref, tri_ref, perm_ref,
                 cum0_ref, tail0_ref, *refs, tile, sub, meta):
    if meta:
        ka_ref, vt_ref, cum_ref, tail_ref, cu_scr, carry_scr = refs
    else:
        qt_ref, ka_ref, vt_ref, yc_ref, cu_scr, carry_scr = refs

    @pl.when(pl.program_id(1) == 0)
    def _():
        carry_scr[...] = jnp.broadcast_to(cum0_ref[:, N_META - 1:N_META], carry_scr.shape)
        cu_scr[0:SUBLANES, :] = tail0_ref[...]

    def rows_at(r0):
        xn = _rms_norm(x_ref[0, r0:r0 + sub], gmix_ref[...]).astype(BF16)

        rows = _dot_nt(wrow_ref[...], xn)
        if not meta:
            qt_ref[0, :, r0:r0 + sub] = rows[0:D_ATTN].astype(BF16)
        vt_ref[0, :, r0:r0 + sub] = rows[D_ATTN:2 * D_ATTN].astype(BF16)
        z = rows[2 * D_ATTN:2 * D_ATTN + N_HEADS] + bf_ref[...]
        log_f = jnp.minimum(z, 0.0) - jnp.log1p(jnp.exp(-jnp.abs(z)))

        c0 = D_ATTN
        cu = (_dot(xn, wtok_ref[:, c0 + D_CONV:c0 + 2 * D_CONV])
              * _dot(xn, wtok_ref[:, c0 + 2 * D_CONV:c0 + 3 * D_CONV]))
        cu_scr[SUBLANES + r0:SUBLANES + r0 + sub, :] = cu

        cum = _split_dot(log_f, tri_ref[...]) + carry_scr[:, 0:1]
        carry_scr[...] = jnp.broadcast_to(cum[:, sub - 1:sub], carry_scr.shape)
        bias = -LOG2E * cum
        if meta:
            cum_ref[...] = cum
            lane = lax.broadcasted_iota(jnp.int32, cum.shape, 1)
            bias = jnp.where(lane < N_META, bias, MASKED)
            tail_ref[...] = cu_scr[N_META:N_META + SUBLANES, :]
        else:
            cu1 = cu_scr[SUBLANES + r0 - 1:SUBLANES + r0 - 1 + sub, :]
            cu2 = cu_scr[SUBLANES + r0 - 2:SUBLANES + r0 - 2 + sub, :]
            conv = convw_ref[0:1, :] * cu2 + convw_ref[1:2, :] * cu1 + convw_ref[2:3, :] * cu
            yc = _dot(xn, wtok_ref[:, c0:c0 + D_CONV]) * conv
        k_lo = _dot(xn, wtok_ref[:, 0:2 * PAIR])
        bias_cols = _dot(perm_ref[...], _stack_terms(bias)).T
        k_hi = _dot(xn, wtok_ref[:, 2 * PAIR:4 * PAIR])
        low = lax.broadcasted_iota(jnp.int32, (1, PAIR), 1) < HEAD_DIM
        if not meta:
            yt = yc.T.reshape(D_CONV // HEAD_DIM, HEAD_DIM, sub)
            ms = jnp.mean(yt * yt, axis=1, keepdims=True)
            yn = (yt * lax.rsqrt(ms + EPS)).reshape(D_CONV, sub).T
            yc_ref[0, r0:r0 + sub] = (yn * gconv_ref[...]).astype(BF16)
        for h in range(N_HEADS):
            kp = (k_lo, k_hi)[h // 4][:, (h // 2 % 2) * PAIR:(h // 2 % 2 + 1) * PAIR]
            ka_ref[0, h, r0:r0 + sub] = jnp.where(low == (h % 2 == 0), kp, bias_cols).astype(BF16)

    for r0 in range(0, tile, sub):
        rows_at(r0)
    if not meta:
        cu_scr[0:SUBLANES, :] = cu_scr[tile:tile + SUBLANES, :]


def _projection(x, gmix, wtok, wrow, bf_col, convw, gconv, cum0, tail0, *, tile, sub, meta):
    nb, seq, _ = x.shape
    nt = seq // tile
    tri = jnp.triu(jnp.ones((sub, sub), F32)).astype(BF16)
    out_lane = jnp.arange(PAIR) % HEAD_DIM
    src_row = jnp.where(out_lane < BIAS_TERMS * N_HEADS, out_lane, -1)
    perm = (src_row[:, None] == jnp.arange(4 * N_HEADS)[None, :]).astype(BF16)
    const = lambda shape: pl.BlockSpec(shape, lambda b, t: (0,) * len(shape))
    in_specs = [
        pl.BlockSpec((1, tile, D_MODEL), lambda b, t: (b, t, 0)),
        const(gmix.shape), const(wtok.shape), const(wrow.shape), const(bf_col.shape), const(convw.shape),
        const(gconv.shape), const(tri.shape), const(perm.shape), const(cum0.shape),
        const(tail0.shape),
    ]
    tok_spec = pl.BlockSpec((1, tile, D_CONV), lambda b, t: (b, t, 0))
    row_spec = pl.BlockSpec((1, D_ATTN, tile), lambda b, t: (b, 0, t))
    ka_spec = pl.BlockSpec((1, N_HEADS, tile, PAIR), lambda b, t: (b, 0, t, 0))
    tok_shape = jax.ShapeDtypeStruct((nb, seq, D_CONV), BF16)
    row_shape = jax.ShapeDtypeStruct((nb, D_ATTN, seq), BF16)
    ka_shape = jax.ShapeDtypeStruct((nb, N_HEADS, seq, PAIR), BF16)
    if meta:
        out_specs = [ka_spec, row_spec, const((N_HEADS, tile)), const((SUBLANES, D_CONV))]
        out_shape = [ka_shape, row_shape, jax.ShapeDtypeStruct((N_HEADS, tile), F32),
                     jax.ShapeDtypeStruct((SUBLANES, D_CONV), F32)]
    else:
        out_specs = [row_spec, ka_spec, row_spec, tok_spec]
        out_shape = [row_shape, ka_shape, row_shape, tok_shape]
    return pl.pallas_call(
        functools.partial(_proj_kernel, tile=tile, sub=sub, meta=meta),
        grid=(nb, nt),
        in_specs=in_specs,
        out_specs=out_specs,
        out_shape=out_shape,
        scratch_shapes=[pltpu.VMEM((tile + SUBLANES, D_CONV), F32), pltpu.VMEM((N_HEADS, LANES), F32)],
        compiler_params=pltpu.CompilerParams(
            dimension_semantics=("arbitrary", "arbitrary"), vmem_limit_bytes=48 * 1024 * 1024),
        name="proj_meta" if meta else "proj",
    )(x, gmix, wtok, wrow, bf_col, convw, gconv, tri, perm, cum0, tail0)


def _attn_kernel(qt_ref, ka_ref, vt_ref, kam_ref, vtm_ref, g_ref, o_ref,
                 m_scr, a_scr, acc_scr, s_scr, p_scr, sm_scr, pm_scr):
    tq, tk = ATTN_TQ, ATTN_TK
    assert tq == 2 * tk, "the pipeline fill assumes two diagonal key tiles per query block"
    n_blocks = qt_ref.shape[2] // tq
    heads = range(N_HEADS)
    real_vt = vt_ref.at[0]
    sub = lax.broadcasted_iota(jnp.int32, (HEAD_DIM, tq), 0)
    ones = jnp.ones((BF16_ROWS, tk), BF16)

    def query(h, q0):
        qh = qt_ref[0, h * HEAD_DIM:(h + 1) * HEAD_DIM, q0:q0 + tq]
        ind = ((sub % N_HEADS == h) & (sub < BIAS_TERMS * N_HEADS)).astype(BF16)
        return jnp.concatenate([qh, ind] if h % 2 == 0 else [ind, qh], axis=0)

    def values(vt, h, k0, width):
        return jnp.concatenate([vt[h * HEAD_DIM:(h + 1) * HEAD_DIM, pl.ds(k0, width)], ones[:, 0:width]], axis=0)

    class Block:
        def __init__(self, i):
            self.q0, self.n_full = i * tq, i * (tq // tk)
            self.qs = [query(h, self.q0) for h in heads]
            w = i % 2
            self.s, self.p = s_scr.at[w], p_scr.at[w]
            self.m, self.a, self.acc = m_scr.at[w], a_scr.at[w], acc_scr.at[w]

        def tile_start(self, j):
            if isinstance(j, int):
                return self.q0 + (1 - j) * tk if j < 2 else (j - 2) * tk
            return pl.multiple_of(jnp.where(j < 2, self.q0 + (1 - j) * tk, (j - 2) * tk), tk)

    def scores(b, h, keys, s_buf, width, diag_offset=None):
        lo = 0 if diag_offset is None else diag_offset
        s = _dot(keys, b.qs[h][:, lo:tq])
        if diag_offset is not None:
            key = lax.broadcasted_iota(jnp.int32, s.shape, 0)
            s = jnp.where(key <= lax.broadcasted_iota(jnp.int32, s.shape, 1), s, MASKED)
        s_buf[h, 0:width, lo:tq] = s

    def softmax(b, h, s_buf, p_buf, width, first=False, lo=0):
        m_new = jnp.max(s_buf[h, 0:width, lo:tq], axis=0, keepdims=True)
        if not first:
            m = b.m[h, :, lo:tq]
            m_new = jnp.maximum(m, m_new)
            b.a[h, :, lo:tq] = jnp.exp2(m - m_new)
        b.m[h, :, lo:tq] = m_new
        p_buf[h, 0:width, lo:tq] = jnp.exp2(s_buf[h, 0:width, lo:tq] - m_new).astype(BF16)
        if lo:
            b.a[h, :, 0:lo] = jnp.ones((1, lo), F32)
            p_buf[h, 0:width, 0:lo] = jnp.zeros((width, lo), BF16)

    def weighted(b, h, vt, p_buf, k0, width, first=False):
        pv = _dot(values(vt, h, k0, width), p_buf[h, 0:width])
        b.acc[h] = pv if first else b.a[h] * b.acc[h] + pv

    def real_keys(h, k0):
        return ka_ref[0, h, pl.ds(k0, tk), :]

    def fill_scores(b, h):
        scores(b, h, kam_ref[h, 0:N_META, :], sm_scr, N_META)
        scores(b, h, real_keys(h, b.q0 + tk), b.s, tk, diag_offset=tk)

    def fill_meta(b, h):
        softmax(b, h, sm_scr, pm_scr, N_META, first=True)

    def fill_rest(b, h):
        weighted(b, h, vtm_ref, pm_scr, 0, N_META, first=True)
        softmax(b, h, b.s, b.p, tk, lo=tk)
        scores(b, h, real_keys(h, b.q0), b.s, tk, diag_offset=0)

    def trip(b, c):
        k_pv, k_qk = b.tile_start(c), pl.multiple_of(c * tk, tk)
        for h in heads:
            weighted(b, h, real_vt, b.p, k_pv, tk)
            softmax(b, h, b.s, b.p, tk)
            scores(b, h, real_keys(h, k_qk), b.s, tk)

    def drain_softmax(b, h):
        weighted(b, h, real_vt, b.p, b.tile_start(b.n_full), tk)
        softmax(b, h, b.s, b.p, tk)

    def drain_values(b, h):
        weighted(b, h, real_vt, b.p, b.tile_start(b.n_full + 1), tk)

    def finish(b):
        def normed(h):
            acc = b.acc[h]
            o = acc[0:HEAD_DIM] / acc[HEAD_DIM:HEAD_DIM + 1]
            return o * lax.rsqrt(jnp.mean(o * o, axis=0, keepdims=True) + EPS)

        for j in range(N_PAIRS):
            y = jnp.concatenate([normed(2 * j), normed(2 * j + 1)], axis=0).T
            o_ref[0, b.q0:b.q0 + tq, j * PAIR:(j + 1) * PAIR] = (
                y * g_ref[:, j * PAIR:(j + 1) * PAIR]).astype(BF16)

    blk = Block(0)
    for h in heads:
        fill_scores(blk, h)
    for h in heads:
        fill_meta(blk, h)
    for h in heads:
        fill_rest(blk, h)
    for i in range(n_blocks):

        def body(c, carry, b=blk):
            for u in range(tq // tk):
                trip(b, (tq // tk) * c + u)
            return carry

        lax.fori_loop(0, jnp.minimum(pl.program_id(0) + i, i), body, 0)
        nxt = Block(i + 1) if i + 1 < n_blocks else None
        for h in heads:
            drain_softmax(blk, h)
            if nxt:
                fill_scores(nxt, h)
        if nxt:
            for h in heads:
                fill_meta(nxt, h)
        for h in heads:
            drain_values(blk, h)
            if nxt:
                fill_rest(nxt, h)
        finish(blk)
        blk = nxt


def _attention(qt, ka, vt, kam, vtm, gattn):
    nb, _, seq = qt.shape
    tq, tk = ATTN_TQ, ATTN_TK
    whole = lambda a: pl.BlockSpec(a.shape, lambda b: (0,) * a.ndim)
    per_batch = lambda a: pl.BlockSpec((1,) + a.shape[1:], lambda b: (b,) + (0,) * (a.ndim - 1))
    return pl.pallas_call(
        _attn_kernel,
        grid=(nb,),
        in_specs=[per_batch(qt), per_batch(ka), per_batch(vt), whole(kam), whole(vtm), whole(gattn)],
        out_specs=pl.BlockSpec((1, seq, D_ATTN), lambda b: (b, 0, 0)),
        out_shape=jax.ShapeDtypeStruct((nb, seq, D_ATTN), BF16),
        scratch_shapes=[
            pltpu.VMEM((2, N_HEADS, 1, tq), F32), pltpu.VMEM((2, N_HEADS, 1, tq), F32),
            pltpu.VMEM((2, N_HEADS, HEAD_DIM + BF16_ROWS, tq), F32),
            pltpu.VMEM((2, N_HEADS, tk, tq), F32), pltpu.VMEM((2, N_HEADS, tk, tq), BF16),
            pltpu.VMEM((N_HEADS, N_META, tq), F32), pltpu.VMEM((N_HEADS, N_META, tq), BF16),
        ],
        compiler_params=pltpu.CompilerParams(
            dimension_semantics=("arbitrary",), vmem_limit_bytes=56 * 1024 * 1024),
        name="attn",
    )(qt, ka, vt, kam, vtm, gattn)


def _mlp_kernel(x_ref, ya_ref, yc_ref, wo_ref, gmlp_ref, w1_ref, w2_ref, gfin_ref, o_ref):
    h = (x_ref[0] + _dot(ya_ref[0], wo_ref[0:D_ATTN, :]) + _dot(yc_ref[0], wo_ref[D_ATTN:D_ATTN + D_CONV, :]))
    hn = _rms_norm(h, gmlp_ref[...]).astype(BF16)
    ff = None
    for c in range(0, D_FF, FF_CHUNK):
        a = jnp.square(jnp.maximum(_dot(hn, w1_ref[:, c:c + FF_CHUNK]), 0.0)).astype(BF16)
        part = _dot(a, w2_ref[c:c + FF_CHUNK, :])
        ff = part if ff is None else ff + part
    o_ref[0] = _rms_norm(h + ff, gfin_ref[...])


def _mlp(x, ya, yc, wo, gmlp, w1, w2, gfin):
    nb, seq, _ = x.shape
    t = MLP_TILE
    const = lambda shape: pl.BlockSpec(shape, lambda b, i: (0,) * len(shape), pipeline_mode=pl.Buffered(1))
    return pl.pallas_call(
        _mlp_kernel,
        grid=(nb, seq // t),
        in_specs=[
            pl.BlockSpec((1, t, D_MODEL), lambda b, i: (b, i, 0)),
            pl.BlockSpec((1, t, D_ATTN), lambda b, i: (b, i, 0)),
            pl.BlockSpec((1, t, D_CONV), lambda b, i: (b, i, 0)),
            const(wo.shape), const(gmlp.shape), const(w1.shape), const(w2.shape), const(gfin.shape),
        ],
        out_specs=pl.BlockSpec((1, t, D_MODEL), lambda b, i: (b, i, 0)),
        out_shape=jax.ShapeDtypeStruct((nb, seq, D_MODEL), F32),
        compiler_params=pltpu.CompilerParams(
            dimension_semantics=("parallel", "parallel"), vmem_limit_bytes=56 * 1024 * 1024),
        name="mlp",
    )(x, ya, yc, wo, gmlp, w1, w2, gfin)


def kernel(x, meta_tokens, norm_mix_g, w_in, b_f, conv_w, out_norm_g, w_out, norm_mlp_g, w_ff1, w_ff2,
           final_norm_g):
    assert w_in.shape[0] == 1, "single-layer block"
    w = w_in[0]
    o_q, o_k, o_v, o_f = 0, D_ATTN, 2 * D_ATTN, 3 * D_ATTN
    o_b = o_f + N_HEADS
    wq = w[:, o_q:o_k] * (LOG2E * HEAD_DIM ** -0.5)
    wtok = jnp.concatenate([w[:, o_k:o_v], w[:, o_b:]], axis=1).astype(BF16)
    wrow = jnp.concatenate([wq, w[:, o_v:o_f], w[:, o_f:o_b], jnp.zeros((D_MODEL, SUBLANES), F32)], axis=1)
    wrow = wrow.T.astype(BF16)
    gmix = norm_mix_g[0][None, :]
    bf_col = b_f[0][:, None]
    convw = jnp.concatenate([conv_w[0], jnp.zeros((SUBLANES - CONV_WIDTH, D_CONV), F32)], axis=0)
    gattn = out_norm_g[0, :D_ATTN][None, :]
    gconv = out_norm_g[0, D_ATTN:][None, :]

    meta = jnp.zeros((1, META_PAD, D_MODEL), F32).at[0, :N_META].set(meta_tokens)
    proj = functools.partial(_projection, gmix=gmix, wtok=wtok, wrow=wrow, bf_col=bf_col, convw=convw,
                             gconv=gconv)
    kam, vtm, cumm, tailm = proj(meta, cum0=jnp.zeros((N_HEADS, LANES), F32),
                                 tail0=jnp.zeros((SUBLANES, D_CONV), F32), tile=META_PAD, sub=META_PAD,
                                 meta=True)
    qt, ka, vt, yc = proj(x, cum0=cumm, tail0=tailm, tile=PROJ_TILE, sub=PROJ_SUB, meta=False)

    ya = _attention(qt, ka, vt, kam[0], vtm[0], gattn)

    return _mlp(x, ya, yc, w_out[0].astype(BF16), norm_mlp_g[0][None, :], w_ff1[0].astype(BF16),
                w_ff2[0].astype(BF16), final_norm_g[None, :])
```

```python
import functools

import jax
import jax.numpy as jnp
from jax import lax
from jax.experimental import pallas as pl
from jax.experimental.pallas import tpu as pltpu

D_MODEL = 1024
N_META = 16
HEAD_DIM = 64
D_ATTN = 512
N_HEADS = D_ATTN // HEAD_DIM
D_CONV = 512
CONV_WIDTH = 3
D_FF = 4096
EPS = 1e-5
MASKED = -1e30
LOG2E = 1.4426950408889634
BIAS_TERMS = 3

LANES = 128
SUBLANES = 8
BF16_ROWS = 16
PAIR = 2 * HEAD_DIM
N_PAIRS = N_HEADS // 2
META_PAD = LANES

PROJ_TILE = 1024
PROJ_SUB = 512
ATTN_TQ = 512
ATTN_TK = 256
MLP_TILE = 512
FF_CHUNK = 1024

BF16 = jnp.bfloat16
F32 = jnp.float32


def _dot(a, b):
    return jnp.dot(a, b, preferred_element_type=F32)


def _dot_nt(a, b):
    return lax.dot_general(a, b, (((1,), (1,)), ((), ())), preferred_element_type=F32)


def _rms_norm(x, g):
    ms = jnp.mean(x * x, axis=-1, keepdims=True)
    return x * lax.rsqrt(ms + EPS) * g


def _split_bf16(x, parts):
    out = []
    r = x
    for _ in range(parts):
        t = r.astype(BF16)
        out.append(t)
        r = r - t.astype(F32)
    return out


def _stack_terms(x):
    terms = [t.astype(F32) for t in _split_bf16(x, BIAS_TERMS)]
    return jnp.concatenate(terms + [jnp.zeros_like(x)], axis=0).astype(BF16)


def _split_dot(x, w):
    sums = _dot(_stack_terms(x), w)
    h = x.shape[0]
    return sums[0:h] + sums[h:2 * h] + sums[2 * h:3 * h]


def _proj_kernel(x_ref, gmix_ref, wtok_ref, wrow_ref, bf_ref, convw_ref, gconv_ref, tri_ref, perm_ref,
                 cum0_ref, tail0_ref, *refs, tile, sub, meta):
    if meta:
        ka_ref, vt_ref, cum_ref, tail_ref, cu_scr, carry_scr = refs
    else:
        qt_ref, ka_ref, vt_ref, yc_ref, cu_scr, carry_scr = refs

    @pl.when(pl.program_id(1) == 0)
    def _():
        carry_scr[...] = jnp.broadcast_to(cum0_ref[:, N_META - 1:N_META], carry_scr.shape)
        cu_scr[0:SUBLANES, :] = tail0_ref[...]

    def rows_at(r0):
        xn = _rms_norm(x_ref[0, r0:r0 + sub], gmix_ref[...]).astype(BF16)

        rows = _dot_nt(wrow_ref[...], xn)
        if not meta:
            qt_ref[0, :, r0:r0 + sub] = rows[0:D_ATTN].astype(BF16)
        vt_ref[0, :, r0:r0 + sub] = rows[D_ATTN:2 * D_ATTN].astype(BF16)
        z = rows[2 * D_ATTN:2 * D_ATTN + N_HEADS] + bf_ref[...]
        log_f = jnp.minimum(z, 0.0) - jnp.log1p(jnp.exp(-jnp.abs(z)))

        c0 = D_ATTN
        cu = (_dot(xn, wtok_ref[:, c0 + D_CONV:c0 + 2 * D_CONV])
              * _dot(xn, wtok_ref[:, c0 + 2 * D_CONV:c0 + 3 * D_CONV]))
        cu_scr[SUBLANES + r0:SUBLANES + r0 + sub, :] = cu

        cum = _split_dot(log_f, tri_ref[...]) + carry_scr[:, 0:1]
        carry_scr[...] = jnp.broadcast_to(cum[:, sub - 1:sub], carry_scr.shape)
        bias = -LOG2E * cum
        if meta:
            cum_ref[...] = cum
            lane = lax.broadcasted_iota(jnp.int32, cum.shape, 1)
            bias = jnp.where(lane < N_META, bias, MASKED)
            tail_ref[...] = cu_scr[N_META:N_META + SUBLANES, :]
        else:
            cu1 = cu_scr[SUBLANES + r0 - 1:SUBLANES + r0 - 1 + sub, :]
            cu2 = cu_scr[SUBLANES + r0 - 2:SUBLANES + r0 - 2 + sub, :]
            conv = convw_ref[0:1, :] * cu2 + convw_ref[1:2, :] * cu1 + convw_ref[2:3, :] * cu
            yc = _dot(xn, wtok_ref[:, c0:c0 + D_CONV]) * conv
        k_lo = _dot(xn, wtok_ref[:, 0:2 * PAIR])
        bias_cols = _dot(perm_ref[...], _stack_terms(bias)).T
        k_hi = _dot(xn, wtok_ref[:, 2 * PAIR:4 * PAIR])
        low = lax.broadcasted_iota(jnp.int32, (1, PAIR), 1) < HEAD_DIM
        if not meta:
            yt = yc.T.reshape(D_CONV // HEAD_DIM, HEAD_DIM, sub)
            ms = jnp.mean(yt * yt, axis=1, keepdims=True)
            yn = (yt * lax.rsqrt(ms + EPS)).reshape(D_CONV, sub).T
            yc_ref[0, r0:r0 + sub] = (yn * gconv_ref[...]).astype(BF16)
        for h in range(N_HEADS):
            kp = (k_lo, k_hi)[h // 4][:, (h // 2 % 2) * PAIR:(h // 2 % 2 + 1) * PAIR]
            ka_ref[0, h, r0:r0 + sub] = jnp.where(low == (h % 2 == 0), kp, bias_cols).astype(BF16)

    for r0 in range(0, tile, sub):
        rows_at(r0)
    if not meta:
        cu_scr[0:SUBLANES, :] = cu_scr[tile:tile + SUBLANES, :]


def _projection(x, gmix, wtok, wrow, bf_col, convw, gconv, cum0, tail0, *, tile, sub, meta):
    nb, seq, _ = x.shape
    nt = seq // tile
    tri = jnp.triu(jnp.ones((sub, sub), F32)).astype(BF16)
    out_lane = jnp.arange(PAIR) % HEAD_DIM
    src_row = jnp.where(out_lane < BIAS_TERMS * N_HEADS, out_lane, -1)
    perm = (src_row[:, None] == jnp.arange(4 * N_HEADS)[None, :]).astype(BF16)
    const = lambda shape: pl.BlockSpec(shape, lambda b, t: (0,) * len(shape))
    in_specs = [
        pl.BlockSpec((1, tile, D_MODEL), lambda b, t: (b, t, 0)),
        const(gmix.shape), const(wtok.shape), const(wrow.shape), const(bf_col.shape), const(convw.shape),
        const(gconv.shape), const(tri.shape), const(perm.shape), const(cum0.shape),
        const(tail0.shape),
    ]
    tok_spec = pl.BlockSpec((1, tile, D_CONV), lambda b, t: (b, t, 0))
    row_spec = pl.BlockSpec((1, D_ATTN, tile), lambda b, t: (b, 0, t))
    ka_spec = pl.BlockSpec((1, N_HEADS, tile, PAIR), lambda b, t: (b, 0, t, 0))
    tok_shape = jax.ShapeDtypeStruct((nb, seq, D_CONV), BF16)
    row_shape = jax.ShapeDtypeStruct((nb, D_ATTN, seq), BF16)
    ka_shape = jax.ShapeDtypeStruct((nb, N_HEADS, seq, PAIR), BF16)
    if meta:
        out_specs = [ka_spec, row_spec, const((N_HEADS, tile)), const((SUBLANES, D_CONV))]
        out_shape = [ka_shape, row_shape, jax.ShapeDtypeStruct((N_HEADS, tile), F32),
                     jax.ShapeDtypeStruct((SUBLANES, D_CONV), F32)]
    else:
        out_specs = [row_spec, ka_spec, row_spec, tok_spec]
        out_shape = [row_shape, ka_shape, row_shape, tok_shape]
    return pl.pallas_call(
        functools.partial(_proj_kernel, tile=tile, sub=sub, meta=meta),
        grid=(nb, nt),
        in_specs=in_specs,
        out_specs=out_specs,
        out_shape=out_shape,
        scratch_shapes=[pltpu.VMEM((tile + SUBLANES, D_CONV), F32), pltpu.VMEM((N_HEADS, LANES), F32)],
        compiler_params=pltpu.CompilerParams(
            dimension_semantics=("arbitrary", "arbitrary"), vmem_limit_bytes=48 * 1024 * 1024),
        name="proj_meta" if meta else "proj",
    )(x, gmix, wtok, wrow, bf_col, convw, gconv, tri, perm, cum0, tail0)


def _attn_kernel(qt_ref, ka_ref, vt_ref, kam_ref, vtm_ref, g_ref, wo_ref, w1_ref, w2_ref,
                 o_ref, wo_bf_ref, w1_bf_ref, w2_bf_ref,
                 m_scr, a_scr, acc_scr, s_scr, p_scr, sm_scr, pm_scr):
    tq, tk = ATTN_TQ, ATTN_TK
    assert tq == 2 * tk, "the pipeline fill assumes two diagonal key tiles per query block"
    for src, dst in ((wo_ref, wo_bf_ref), (w1_ref, w1_bf_ref), (w2_ref, w2_bf_ref)):
        dst[...] = src[...].astype(BF16)
    n_blocks = qt_ref.shape[2] // tq
    heads = range(N_HEADS)
    real_vt = vt_ref.at[0]
    sub = lax.broadcasted_iota(jnp.int32, (HEAD_DIM, tq), 0)
    ones = jnp.ones((BF16_ROWS, tk), BF16)

    def query(h, q0):
        qh = qt_ref[0, h * HEAD_DIM:(h + 1) * HEAD_DIM, q0:q0 + tq]
        ind = ((sub % N_HEADS == h) & (sub < BIAS_TERMS * N_HEADS)).astype(BF16)
        return jnp.concatenate([qh, ind] if h % 2 == 0 else [ind, qh], axis=0)

    def values(vt, h, k0, width):
        return jnp.concatenate([vt[h * HEAD_DIM:(h + 1) * HEAD_DIM, pl.ds(k0, width)], ones[:, 0:width]], axis=0)

    class Block:
        def __init__(self, i):
            self.q0, self.n_full = i * tq, i * (tq // tk)
            self.qs = [query(h, self.q0) for h in heads]
            w = i % 2
            self.s, self.p = s_scr.at[w], p_scr.at[w]
            self.m, self.a, self.acc = m_scr.at[w], a_scr.at[w], acc_scr.at[w]

        def tile_start(self, j):
            if isinstance(j, int):
                return self.q0 + (1 - j) * tk if j < 2 else (j - 2) * tk
            return pl.multiple_of(jnp.where(j < 2, self.q0 + (1 - j) * tk, (j - 2) * tk), tk)

    def scores(b, h, keys, s_buf, width, diag_offset=None):
        lo = 0 if diag_offset is None else diag_offset
        s = _dot(keys, b.qs[h][:, lo:tq])
        if diag_offset is not None:
            key = lax.broadcasted_iota(jnp.int32, s.shape, 0)
            s = jnp.where(key <= lax.broadcasted_iota(jnp.int32, s.shape, 1), s, MASKED)
        s_buf[h, 0:width, lo:tq] = s

    def softmax(b, h, s_buf, p_buf, width, first=False, lo=0):
        m_new = jnp.max(s_buf[h, 0:width, lo:tq], axis=0, keepdims=True)
        if not first:
            m = b.m[h, :, lo:tq]
            m_new = jnp.maximum(m, m_new)
            b.a[h, :, lo:tq] = jnp.exp2(m - m_new)
        b.m[h, :, lo:tq] = m_new
        p_buf[h, 0:width, lo:tq] = jnp.exp2(s_buf[h, 0:width, lo:tq] - m_new).astype(BF16)
        if lo:
            b.a[h, :, 0:lo] = jnp.ones((1, lo), F32)
            p_buf[h, 0:width, 0:lo] = jnp.zeros((width, lo), BF16)

    def weighted(b, h, vt, p_buf, k0, width, first=False):
        pv = _dot(values(vt, h, k0, width), p_buf[h, 0:width])
        b.acc[h] = pv if first else b.a[h] * b.acc[h] + pv

    def real_keys(h, k0):
        return ka_ref[0, h, pl.ds(k0, tk), :]

    def fill_scores(b, h):
        scores(b, h, kam_ref[h, 0:N_META, :], sm_scr, N_META)
        scores(b, h, real_keys(h, b.q0 + tk), b.s, tk, diag_offset=tk)

    def fill_meta(b, h):
        softmax(b, h, sm_scr, pm_scr, N_META, first=True)

    def fill_rest(b, h):
        weighted(b, h, vtm_ref, pm_scr, 0, N_META, first=True)
        softmax(b, h, b.s, b.p, tk, lo=tk)
        scores(b, h, real_keys(h, b.q0), b.s, tk, diag_offset=0)

    def trip(b, c):
        k_pv, k_qk = b.tile_start(c), pl.multiple_of(c * tk, tk)
        for h in heads:
            weighted(b, h, real_vt, b.p, k_pv, tk)
            softmax(b, h, b.s, b.p, tk)
            scores(b, h, real_keys(h, k_qk), b.s, tk)

    def drain_softmax(b, h):
        weighted(b, h, real_vt, b.p, b.tile_start(b.n_full), tk)
        softmax(b, h, b.s, b.p, tk)

    def drain_values(b, h):
        weighted(b, h, real_vt, b.p, b.tile_start(b.n_full + 1), tk)

    def finish(b):
        def normed(h):
            acc = b.acc[h]
            o = acc[0:HEAD_DIM] / acc[HEAD_DIM:HEAD_DIM + 1]
            return o * lax.rsqrt(jnp.mean(o * o, axis=0, keepdims=True) + EPS)

        for j in range(N_PAIRS):
            y = jnp.concatenate([normed(2 * j), normed(2 * j + 1)], axis=0).T
            o_ref[0, b.q0:b.q0 + tq, j * PAIR:(j + 1) * PAIR] = (
                y * g_ref[:, j * PAIR:(j + 1) * PAIR]).astype(BF16)

    blk = Block(0)
    for h in heads:
        fill_scores(blk, h)
    for h in heads:
        fill_meta(blk, h)
    for h in heads:
        fill_rest(blk, h)
    for i in range(n_blocks):

        def body(c, carry, b=blk):
            for u in range(tq // tk):
                trip(b, (tq // tk) * c + u)
            return carry

        lax.fori_loop(0, jnp.minimum(pl.program_id(0) + i, i), body, 0)
        nxt = Block(i + 1) if i + 1 < n_blocks else None
        for h in heads:
            drain_softmax(blk, h)
            if nxt:
                fill_scores(nxt, h)
        if nxt:
            for h in heads:
                fill_meta(nxt, h)
        for h in heads:
            drain_values(blk, h)
            if nxt:
                fill_rest(nxt, h)
        finish(blk)
        blk = nxt


def _attention(qt, ka, vt, kam, vtm, gattn, mlp_weights):
    nb, _, seq = qt.shape
    tq, tk = ATTN_TQ, ATTN_TK
    whole = lambda a: pl.BlockSpec(a.shape, lambda b: (0,) * a.ndim)
    per_batch = lambda a: pl.BlockSpec((1,) + a.shape[1:], lambda b: (b,) + (0,) * (a.ndim - 1))
    assert all(w.shape[0] % (nb * BF16_ROWS) == 0 for w in mlp_weights)
    row_slice = lambda w: pl.BlockSpec((w.shape[0] // nb, w.shape[1]), lambda b: (b, 0))
    return pl.pallas_call(
        _attn_kernel,
        grid=(nb,),
        in_specs=[per_batch(qt), per_batch(ka), per_batch(vt), whole(kam), whole(vtm), whole(gattn)]
        + [row_slice(w) for w in mlp_weights],
        out_specs=[pl.BlockSpec((1, seq, D_ATTN), lambda b: (b, 0, 0))] + [row_slice(w) for w in mlp_weights],
        out_shape=[jax.ShapeDtypeStruct((nb, seq, D_ATTN), BF16)]
        + [jax.ShapeDtypeStruct(w.shape, BF16) for w in mlp_weights],
        scratch_shapes=[
            pltpu.VMEM((2, N_HEADS, 1, tq), F32), pltpu.VMEM((2, N_HEADS, 1, tq), F32),
            pltpu.VMEM((2, N_HEADS, HEAD_DIM + BF16_ROWS, tq), F32),
            pltpu.VMEM((2, N_HEADS, tk, tq), F32), pltpu.VMEM((2, N_HEADS, tk, tq), BF16),
            pltpu.VMEM((N_HEADS, N_META, tq), F32), pltpu.VMEM((N_HEADS, N_META, tq), BF16),
        ],
        compiler_params=pltpu.CompilerParams(
            dimension_semantics=("arbitrary",), vmem_limit_bytes=56 * 1024 * 1024),
        name="attn",
    )(qt, ka, vt, kam, vtm, gattn, *mlp_weights)


def _mlp_kernel(x_ref, ya_ref, yc_ref, wo_ref, gmlp_ref, w1_ref, w2_ref, gfin_ref, o_ref):
    h = (x_ref[0] + _dot(ya_ref[0], wo_ref[0:D_ATTN, :]) + _dot(yc_ref[0], wo_ref[D_ATTN:D_ATTN + D_CONV, :]))
    hn = _rms_norm(h, gmlp_ref[...]).astype(BF16)
    ff = None
    for c in range(0, D_FF, FF_CHUNK):
        a = jnp.square(jnp.maximum(_dot(hn, w1_ref[:, c:c + FF_CHUNK]), 0.0)).astype(BF16)
        part = _dot(a, w2_ref[c:c + FF_CHUNK, :])
        ff = part if ff is None else ff + part
    o_ref[0] = _rms_norm(h + ff, gfin_ref[...])


def _mlp(x, ya, yc, wo, gmlp, w1, w2, gfin):
    nb, seq, _ = x.shape
    t = MLP_TILE
    const = lambda shape: pl.BlockSpec(shape, lambda b, i: (0,) * len(shape), pipeline_mode=pl.Buffered(1))
    return pl.pallas_call(
        _mlp_kernel,
        grid=(nb, seq // t),
        in_specs=[
            pl.BlockSpec((1, t, D_MODEL), lambda b, i: (b, i, 0)),
            pl.BlockSpec((1, t, D_ATTN), lambda b, i: (b, i, 0)),
            pl.BlockSpec((1, t, D_CONV), lambda b, i: (b, i, 0)),
            const(wo.shape), const(gmlp.shape), const(w1.shape), const(w2.shape), const(gfin.shape),
        ],
        out_specs=pl.BlockSpec((1, t, D_MODEL), lambda b, i: (b, i, 0)),
        out_shape=jax.ShapeDtypeStruct((nb, seq, D_MODEL), F32),
        compiler_params=pltpu.CompilerParams(
            dimension_semantics=("parallel", "parallel"), vmem_limit_bytes=56 * 1024 * 1024),
        name="mlp",
    )(x, ya, yc, wo, gmlp, w1, w2, gfin)


def kernel(x, meta_tokens, norm_mix_g, w_in, b_f, conv_w, out_norm_g, w_out, norm_mlp_g, w_ff1, w_ff2,
           final_norm_g):
    assert w_in.shape[0] == 1, "single-layer block"
    w = w_in[0]
    o_q, o_k, o_v, o_f = 0, D_ATTN, 2 * D_ATTN, 3 * D_ATTN
    o_b = o_f + N_HEADS
    wq = w[:, o_q:o_k] * (LOG2E * HEAD_DIM ** -0.5)
    wtok = jnp.concatenate([w[:, o_k:o_v], w[:, o_b:]], axis=1).astype(BF16)
    wrow = jnp.concatenate([wq, w[:, o_v:o_f], w[:, o_f:o_b], jnp.zeros((D_MODEL, SUBLANES), F32)], axis=1)
    wrow = wrow.T.astype(BF16)
    gmix = norm_mix_g[0][None, :]
    bf_col = b_f[0][:, None]
    convw = jnp.concatenate([conv_w[0], jnp.zeros((SUBLANES - CONV_WIDTH, D_CONV), F32)], axis=0)
    gattn = out_norm_g[0, :D_ATTN][None, :]
    gconv = out_norm_g[0, D_ATTN:][None, :]

    meta = jnp.zeros((1, META_PAD, D_MODEL), F32).at[0, :N_META].set(meta_tokens)
    proj = functools.partial(_projection, gmix=gmix, wtok=wtok, wrow=wrow, bf_col=bf_col, convw=convw,
                             gconv=gconv)
    kam, vtm, cumm, tailm = proj(meta, cum0=jnp.zeros((N_HEADS, LANES), F32),
                                 tail0=jnp.zeros((SUBLANES, D_CONV), F32), tile=META_PAD, sub=META_PAD,
                                 meta=True)
    qt, ka, vt, yc = proj(x, cum0=cumm, tail0=tailm, tile=PROJ_TILE, sub=PROJ_SUB, meta=False)

    ya, wo, w1, w2 = _attention(qt, ka, vt, kam[0], vtm[0], gattn, (w_out[0], w_ff1[0], w_ff2[0]))

    return _mlp(x, ya, yc, wo, norm_mlp_g[0][None, :], w1, w2, final_norm_g[None, :])
```

```python
import functools

import jax
import jax.numpy as jnp
import numpy as np
from jax import lax
from jax.experimental import pallas as pl
from jax.experimental.pallas import tpu as pltpu

D_MODEL = 1024
N_META = 16
HEAD_DIM = 64
D_ATTN = 512
N_HEADS = D_ATTN // HEAD_DIM
D_CONV = 512
CONV_WIDTH = 3
D_FF = 4096
EPS = 1e-5
MASKED = -1e30
LOG2E = 1.4426950408889634
BIAS_TERMS = 3

LANES = 128
SUBLANES = 8
BF16_ROWS = 16
PAIR = 2 * HEAD_DIM
N_PAIRS = N_HEADS // 2
META_PAD = LANES

PROJ_TILE = 1024
PROJ_SUB = 512
ATTN_TQ = 512
ATTN_TK = 256
MLP_TILE = 512
FF_CHUNK = 1024

BF16 = jnp.bfloat16
F32 = jnp.float32


def _dot(a, b):
    return jnp.dot(a, b, preferred_element_type=F32)


def _dot_tn(a, b):
    return lax.dot_general(a, b, (((0,), (1,)), ((), ())), preferred_element_type=F32)


def _rms_norm(x, g):
    ms = jnp.mean(x * x, axis=-1, keepdims=True)
    return x * lax.rsqrt(ms + EPS) * g


def _split_bf16(x, parts):
    out = []
    r = x
    for _ in range(parts):
        t = r.astype(BF16)
        out.append(t)
        r = r - t.astype(F32)
    return out


def _stack_terms(x):
    terms = [t.astype(F32) for t in _split_bf16(x, BIAS_TERMS)]
    return jnp.concatenate(terms + [jnp.zeros_like(x)], axis=0).astype(BF16)


def _split_dot(x, w):
    sums = _dot(_stack_terms(x), w)
    h = x.shape[0]
    return sums[0:h] + sums[h:2 * h] + sums[2 * h:3 * h]


def _proj_kernel(x_ref, gmix_ref, wtok_ref, wrow_ref, bf_ref, convw_ref, gconv_ref, tri_ref, perm_ref,
                 cum0_ref, tail0_ref, *refs, tile, sub, meta):
    if meta:
        ka_ref, vt_ref, cum_ref, tail_ref, cu_scr, carry_scr = refs
    else:
        qt_ref, ka_ref, vt_ref, yc_ref, cu_scr, carry_scr = refs

    @pl.when(pl.program_id(1) == 0)
    def _():
        carry_scr[...] = jnp.broadcast_to(cum0_ref[:, N_META - 1:N_META], carry_scr.shape)
        cu_scr[0:SUBLANES, :] = tail0_ref[...]

    def rows_at(r0):
        xn = _rms_norm(x_ref[0, r0:r0 + sub], gmix_ref[...]).astype(BF16)

        rows = _dot_tn(wrow_ref[...], xn)
        if not meta:
            qt_ref[0, :, r0:r0 + sub] = rows[0:D_ATTN].astype(BF16)
        vt_ref[0, :, r0:r0 + sub] = rows[D_ATTN:2 * D_ATTN].astype(BF16)
        z = rows[2 * D_ATTN:2 * D_ATTN + N_HEADS] + bf_ref[...]
        log_f = jnp.minimum(z, 0.0) - jnp.log1p(jnp.exp(-jnp.abs(z)))

        c0 = D_ATTN
        cu = (_dot(xn, wtok_ref[:, c0 + D_CONV:c0 + 2 * D_CONV])
              * _dot(xn, wtok_ref[:, c0 + 2 * D_CONV:c0 + 3 * D_CONV]))
        cu_scr[SUBLANES + r0:SUBLANES + r0 + sub, :] = cu

        cum = _split_dot(log_f, tri_ref[...]) + carry_scr[:, 0:1]
        carry_scr[...] = jnp.broadcast_to(cum[:, sub - 1:sub], carry_scr.shape)
        bias = -LOG2E * cum
        if meta:
            cum_ref[...] = cum
            lane = lax.broadcasted_iota(jnp.int32, cum.shape, 1)
            bias = jnp.where(lane < N_META, bias, MASKED)
            tail_ref[...] = cu_scr[N_META:N_META + SUBLANES, :]
        else:
            cu1 = cu_scr[SUBLANES + r0 - 1:SUBLANES + r0 - 1 + sub, :]
            cu2 = cu_scr[SUBLANES + r0 - 2:SUBLANES + r0 - 2 + sub, :]
            conv = convw_ref[0:1, :] * cu2 + convw_ref[1:2, :] * cu1 + convw_ref[2:3, :] * cu
            yc = _dot(xn, wtok_ref[:, c0:c0 + D_CONV]) * conv
        k_lo = _dot(xn, wtok_ref[:, 0:2 * PAIR])
        bias_cols = _dot(perm_ref[...], _stack_terms(bias)).T
        k_hi = _dot(xn, wtok_ref[:, 2 * PAIR:4 * PAIR])
        low = lax.broadcasted_iota(jnp.int32, (1, PAIR), 1) < HEAD_DIM
        if not meta:
            yt = yc.T.reshape(D_CONV // HEAD_DIM, HEAD_DIM, sub)
            ms = jnp.mean(yt * yt, axis=1, keepdims=True)
            yn = (yt * lax.rsqrt(ms + EPS)).reshape(D_CONV, sub).T
            yc_ref[0, r0:r0 + sub] = (yn * gconv_ref[...]).astype(BF16)
        for h in range(N_HEADS):
            kp = (k_lo, k_hi)[h // 4][:, (h // 2 % 2) * PAIR:(h // 2 % 2 + 1) * PAIR]
            ka_ref[0, h, r0:r0 + sub] = jnp.where(low == (h % 2 == 0), kp, bias_cols).astype(BF16)

    for r0 in range(0, tile, sub):
        rows_at(r0)
    if not meta:
        cu_scr[0:SUBLANES, :] = cu_scr[tile:tile + SUBLANES, :]


def _projection(x, gmix, wtok, wrow, bf_col, convw, gconv, cum0, tail0, *, tile, sub, meta):
    nb, seq, _ = x.shape
    nt = seq // tile
    tri = jnp.asarray(np.triu(np.ones((sub, sub), np.float32)), BF16)
    out_lane = np.arange(PAIR) % HEAD_DIM
    src_row = np.where(out_lane < BIAS_TERMS * N_HEADS, out_lane, -1)
    perm = jnp.asarray(src_row[:, None] == np.arange(4 * N_HEADS)[None, :], BF16)
    const = lambda shape: pl.BlockSpec(shape, lambda b, t: (0,) * len(shape))
    in_specs = [
        pl.BlockSpec((1, tile, D_MODEL), lambda b, t: (b, t, 0)),
        const(gmix.shape), const(wtok.shape), const(wrow.shape), const(bf_col.shape), const(convw.shape),
        const(gconv.shape), const(tri.shape), const(perm.shape), const(cum0.shape),
        const(tail0.shape),
    ]
    tok_spec = pl.BlockSpec((1, tile, D_CONV), lambda b, t: (b, t, 0))
    row_spec = pl.BlockSpec((1, D_ATTN, tile), lambda b, t: (b, 0, t))
    ka_spec = pl.BlockSpec((1, N_HEADS, tile, PAIR), lambda b, t: (b, 0, t, 0))
    tok_shape = jax.ShapeDtypeStruct((nb, seq, D_CONV), BF16)
    row_shape = jax.ShapeDtypeStruct((nb, D_ATTN, seq), BF16)
    ka_shape = jax.ShapeDtypeStruct((nb, N_HEADS, seq, PAIR), BF16)
    if meta:
        out_specs = [ka_spec, row_spec, const((N_HEADS, tile)), const((SUBLANES, D_CONV))]
        out_shape = [ka_shape, row_shape, jax.ShapeDtypeStruct((N_HEADS, tile), F32),
                     jax.ShapeDtypeStruct((SUBLANES, D_CONV), F32)]
    else:
        out_specs = [row_spec, ka_spec, row_spec, tok_spec]
        out_shape = [row_shape, ka_shape, row_shape, tok_shape]
    return pl.pallas_call(
        functools.partial(_proj_kernel, tile=tile, sub=sub, meta=meta),
        grid=(nb, nt),
        in_specs=in_specs,
        out_specs=out_specs,
        out_shape=out_shape,
        scratch_shapes=[pltpu.VMEM((tile + SUBLANES, D_CONV), F32), pltpu.VMEM((N_HEADS, LANES), F32)],
        compiler_params=pltpu.CompilerParams(
            dimension_semantics=("arbitrary", "arbitrary"), vmem_limit_bytes=48 * 1024 * 1024),
        name="proj_meta" if meta else "proj",
    )(x, gmix, wtok, wrow, bf_col, convw, gconv, tri, perm, cum0, tail0)


def _attn_kernel(qt_ref, ka_ref, vt_ref, kam_ref, vtm_ref, g_ref, wo_ref, w1_ref, w2_ref,
                 o_ref, wo_bf_ref, w1_bf_ref, w2_bf_ref,
                 m_scr, a_scr, acc_scr, s_scr, p_scr, sm_scr, pm_scr):
    tq, tk = ATTN_TQ, ATTN_TK
    assert tq == 2 * tk, "the pipeline fill assumes two diagonal key tiles per query block"
    for src, dst in ((wo_ref, wo_bf_ref), (w1_ref, w1_bf_ref), (w2_ref, w2_bf_ref)):
        dst[...] = src[...].astype(BF16)
    n_blocks = qt_ref.shape[2] // tq
    heads = range(N_HEADS)
    real_vt = vt_ref.at[0]
    sub = lax.broadcasted_iota(jnp.int32, (HEAD_DIM, tq), 0)
    ones = jnp.ones((BF16_ROWS, tk), BF16)

    def query(h, q0):
        qh = qt_ref[0, h * HEAD_DIM:(h + 1) * HEAD_DIM, q0:q0 + tq]
        ind = ((sub % N_HEADS == h) & (sub < BIAS_TERMS * N_HEADS)).astype(BF16)
        return jnp.concatenate([qh, ind] if h % 2 == 0 else [ind, qh], axis=0)

    def values(vt, h, k0, width):
        return jnp.concatenate([vt[h * HEAD_DIM:(h + 1) * HEAD_DIM, pl.ds(k0, width)], ones[:, 0:width]], axis=0)

    class Block:
        def __init__(self, i):
            self.q0, self.n_full = i * tq, i * (tq // tk)
            self.qs = [query(h, self.q0) for h in heads]
            w = i % 2
            self.s, self.p = s_scr.at[w], p_scr.at[w]
            self.m, self.a, self.acc = m_scr.at[w], a_scr.at[w], acc_scr.at[w]

        def tile_start(self, j):
            if isinstance(j, int):
                return self.q0 + (1 - j) * tk if j < 2 else (j - 2) * tk
            return pl.multiple_of(jnp.where(j < 2, self.q0 + (1 - j) * tk, (j - 2) * tk), tk)

    def scores(b, h, keys, s_buf, width, diag_offset=None):
        lo = 0 if diag_offset is None else diag_offset
        s = _dot(keys, b.qs[h][:, lo:tq])
        if diag_offset is not None:
            key = lax.broadcasted_iota(jnp.int32, s.shape, 0)
            s = jnp.where(key <= lax.broadcasted_iota(jnp.int32, s.shape, 1), s, MASKED)
        s_buf[h, 0:width, lo:tq] = s

    def softmax(b, h, s_buf, p_buf, width, first=False, lo=0):
        m_new = jnp.max(s_buf[h, 0:width, lo:tq], axis=0, keepdims=True)
        if not first:
            m = b.m[h, :, lo:tq]
            m_new = jnp.maximum(m, m_new)
            b.a[h, :, lo:tq] = jnp.exp2(m - m_new)
        b.m[h, :, lo:tq] = m_new
        p_buf[h, 0:width, lo:tq] = jnp.exp2(s_buf[h, 0:width, lo:tq] - m_new).astype(BF16)
        if lo:
            b.a[h, :, 0:lo] = jnp.ones((1, lo), F32)
            p_buf[h, 0:width, 0:lo] = jnp.zeros((width, lo), BF16)

    def weighted(b, h, vt, p_buf, k0, width, first=False):
        pv = _dot(values(vt, h, k0, width), p_buf[h, 0:width])
        b.acc[h] = pv if first else b.a[h] * b.acc[h] + pv

    def real_keys(h, k0):
        return ka_ref[0, h, pl.ds(k0, tk), :]

    def fill_scores(b, h):
        scores(b, h, kam_ref[h, 0:N_META, :], sm_scr, N_META)
        scores(b, h, real_keys(h, b.q0 + tk), b.s, tk, diag_offset=tk)

    def fill_meta(b, h):
        softmax(b, h, sm_scr, pm_scr, N_META, first=True)

    def fill_rest(b, h):
        weighted(b, h, vtm_ref, pm_scr, 0, N_META, first=True)
        softmax(b, h, b.s, b.p, tk, lo=tk)
        scores(b, h, real_keys(h, b.q0), b.s, tk, diag_offset=0)

    def trip(b, c):
        k_pv, k_qk = b.tile_start(c), pl.multiple_of(c * tk, tk)
        for h in heads:
            weighted(b, h, real_vt, b.p, k_pv, tk)
            softmax(b, h, b.s, b.p, tk)
            scores(b, h, real_keys(h, k_qk), b.s, tk)

    def drain_softmax(b, h):
        weighted(b, h, real_vt, b.p, b.tile_start(b.n_full), tk)
        softmax(b, h, b.s, b.p, tk)

    def drain_values(b, h):
        weighted(b, h, real_vt, b.p, b.tile_start(b.n_full + 1), tk)

    def finish(b):
        def normed(h):
            acc = b.acc[h]
            o = acc[0:HEAD_DIM] / acc[HEAD_DIM:HEAD_DIM + 1]
            return o * lax.rsqrt(jnp.mean(o * o, axis=0, keepdims=True) + EPS)

        for j in range(N_PAIRS):
            y = jnp.concatenate([normed(2 * j), normed(2 * j + 1)], axis=0).T
            o_ref[0, b.q0:b.q0 + tq, j * PAIR:(j + 1) * PAIR] = (
                y * g_ref[:, j * PAIR:(j + 1) * PAIR]).astype(BF16)

    blk = Block(0)
    for h in heads:
        fill_scores(blk, h)
    for h in heads:
        fill_meta(blk, h)
    for h in heads:
        fill_rest(blk, h)
    for i in range(n_blocks):

        def body(c, carry, b=blk):
            for u in range(tq // tk):
                trip(b, (tq // tk) * c + u)
            return carry

        lax.fori_loop(0, jnp.minimum(pl.program_id(0) + i, i), body, 0)
        nxt = Block(i + 1) if i + 1 < n_blocks else None
        for h in heads:
            drain_softmax(blk, h)
            if nxt:
                fill_scores(nxt, h)
        if nxt:
            for h in heads:
                fill_meta(nxt, h)
        for h in heads:
            drain_values(blk, h)
            if nxt:
                fill_rest(nxt, h)
        finish(blk)
        blk = nxt


def _attention(qt, ka, vt, kam, vtm, gattn, mlp_weights):
    nb, _, seq = qt.shape
    tq, tk = ATTN_TQ, ATTN_TK
    whole = lambda a: pl.BlockSpec(a.shape, lambda b: (0,) * a.ndim)
    per_batch = lambda a: pl.BlockSpec((1,) + a.shape[1:], lambda b: (b,) + (0,) * (a.ndim - 1))
    assert all(w.shape[0] % (nb * BF16_ROWS) == 0 for w in mlp_weights)
    row_slice = lambda w: pl.BlockSpec((w.shape[0] // nb, w.shape[1]), lambda b: (b, 0))
    return pl.pallas_call(
        _attn_kernel,
        grid=(nb,),
        in_specs=[per_batch(qt), per_batch(ka), per_batch(vt), whole(kam), whole(vtm), whole(gattn)]
        + [row_slice(w) for w in mlp_weights],
        out_specs=[pl.BlockSpec((1, seq, D_ATTN), lambda b: (b, 0, 0))] + [row_slice(w) for w in mlp_weights],
        out_shape=[jax.ShapeDtypeStruct((nb, seq, D_ATTN), BF16)]
        + [jax.ShapeDtypeStruct(w.shape, BF16) for w in mlp_weights],
        scratch_shapes=[
            pltpu.VMEM((2, N_HEADS, 1, tq), F32), pltpu.VMEM((2, N_HEADS, 1, tq), F32),
            pltpu.VMEM((2, N_HEADS, HEAD_DIM + BF16_ROWS, tq), F32),
            pltpu.VMEM((2, N_HEADS, tk, tq), F32), pltpu.VMEM((2, N_HEADS, tk, tq), BF16),
            pltpu.VMEM((N_HEADS, N_META, tq), F32), pltpu.VMEM((N_HEADS, N_META, tq), BF16),
        ],
        compiler_params=pltpu.CompilerParams(
            dimension_semantics=("arbitrary",), vmem_limit_bytes=56 * 1024 * 1024),
        name="attn",
    )(qt, ka, vt, kam, vtm, gattn, *mlp_weights)


def _mlp_kernel(x_ref, ya_ref, yc_ref, wo_ref, gmlp_ref, w1_ref, w2_ref, gfin_ref, o_ref):
    h = (x_ref[0] + _dot(ya_ref[0], wo_ref[0:D_ATTN, :]) + _dot(yc_ref[0], wo_ref[D_ATTN:D_ATTN + D_CONV, :]))
    hn = _rms_norm(h, gmlp_ref[...]).astype(BF16)
    ff = None
    for c in range(0, D_FF, FF_CHUNK):
        a = jnp.square(jnp.maximum(_dot(hn, w1_ref[:, c:c + FF_CHUNK]), 0.0)).astype(BF16)
        part = _dot(a, w2_ref[c:c + FF_CHUNK, :])
        ff = part if ff is None else ff + part
    o_ref[0] = _rms_norm(h + ff, gfin_ref[...])


def _mlp(x, ya, yc, wo, gmlp, w1, w2, gfin):
    nb, seq, _ = x.shape
    t = MLP_TILE
    const = lambda shape: pl.BlockSpec(shape, lambda b, i: (0,) * len(shape), pipeline_mode=pl.Buffered(1))
    return pl.pallas_call(
        _mlp_kernel,
        grid=(nb, seq // t),
        in_specs=[
            pl.BlockSpec((1, t, D_MODEL), lambda b, i: (b, i, 0)),
            pl.BlockSpec((1, t, D_ATTN), lambda b, i: (b, i, 0)),
            pl.BlockSpec((1, t, D_CONV), lambda b, i: (b, i, 0)),
            const(wo.shape), const(gmlp.shape), const(w1.shape), const(w2.shape), const(gfin.shape),
        ],
        out_specs=pl.BlockSpec((1, t, D_MODEL), lambda b, i: (b, i, 0)),
        out_shape=jax.ShapeDtypeStruct((nb, seq, D_MODEL), F32),
        compiler_params=pltpu.CompilerParams(
            dimension_semantics=("parallel", "parallel"), vmem_limit_bytes=56 * 1024 * 1024),
        name="mlp",
    )(x, ya, yc, wo, gmlp, w1, w2, gfin)


def kernel(x, meta_tokens, norm_mix_g, w_in, b_f, conv_w, out_norm_g, w_out, norm_mlp_g, w_ff1, w_ff2,
           final_norm_g):
    assert w_in.shape[0] == 1, "single-layer block"
    w = w_in[0]
    o_q, o_k, o_v, o_f = 0, D_ATTN, 2 * D_ATTN, 3 * D_ATTN
    o_b = o_f + N_HEADS
    wq = w[:, o_q:o_k] * (LOG2E * HEAD_DIM ** -0.5)
    wtok = jnp.concatenate([w[:, o_k:o_v], w[:, o_b:]], axis=1).astype(BF16)
    wrow = jnp.concatenate([wq, w[:, o_v:o_f], w[:, o_f:o_b], jnp.zeros((D_MODEL, SUBLANES), F32)], axis=1)
    wrow = wrow.astype(BF16)
    gmix = norm_mix_g[0][None, :]
    bf_col = b_f[0][:, None]
    convw = jnp.concatenate([conv_w[0], jnp.zeros((SUBLANES - CONV_WIDTH, D_CONV), F32)], axis=0)
    gattn = out_norm_g[0, :D_ATTN][None, :]
    gconv = out_norm_g[0, D_ATTN:][None, :]

    meta = jnp.zeros((1, META_PAD, D_MODEL), F32).at[0, :N_META].set(meta_tokens)
    proj = functools.partial(_projection, gmix=gmix, wtok=wtok, wrow=wrow, bf_col=bf_col, convw=convw,
                             gconv=gconv)
    kam, vtm, cumm, tailm = proj(meta, cum0=jnp.zeros((N_HEADS, LANES), F32),
                                 tail0=jnp.zeros((SUBLANES, D_CONV), F32), tile=META_PAD, sub=META_PAD,
                                 meta=True)
    qt, ka, vt, yc = proj(x, cum0=cumm, tail0=tailm, tile=PROJ_TILE, sub=PROJ_SUB, meta=False)

    ya, wo, w1, w2 = _attention(qt, ka, vt, kam[0], vtm[0], gattn, (w_out[0], w_ff1[0], w_ff2[0]))

    return _mlp(x, ya, yc, wo, norm_mlp_g[0][None, :], w1, w2, final_norm_g[None, :])
```

```python
import functools

import jax
import jax.numpy as jnp
from jax import lax
from jax.experimental import pallas as pl
from jax.experimental.pallas import tpu as pltpu

D_MODEL = 1024
N_META = 16
HEAD_DIM = 64
D_ATTN = 512
N_HEADS = D_ATTN // HEAD_DIM
D_CONV = 512
CONV_WIDTH = 3
D_FF = 4096
EPS = 1e-5
MASKED = -1e30
LOG2E = 1.4426950408889634
BIAS_TERMS = 3

LANES = 128
SUBLANES = 8
BF16_ROWS = 16
PAIR = 2 * HEAD_DIM
N_PAIRS = N_HEADS // 2
META_PAD = LANES

PROJ_TILE = 1024
PROJ_SUB = 512
ATTN_TQ = 512
ATTN_TK = 256
MLP_TILE = 512
FF_CHUNK = 1024

BF16 = jnp.bfloat16
F32 = jnp.float32


def _dot(a, b):
    return jnp.dot(a, b, preferred_element_type=F32)


def _dot_nt(a, b):
    return lax.dot_general(a, b, (((1,), (1,)), ((), ())), preferred_element_type=F32)


def _rms_norm(x, g):
    ms = jnp.mean(x * x, axis=-1, keepdims=True)
    return x * lax.rsqrt(ms + EPS) * g


def _split_bf16(x, parts):
    out = []
    r = x
    for _ in range(parts):
        t = r.astype(BF16)
        out.append(t)
        r = r - t.astype(F32)
    return out


def _stack_terms(x):
    terms = [t.astype(F32) for t in _split_bf16(x, BIAS_TERMS)]
    return jnp.concatenate(terms + [jnp.zeros_like(x)], axis=0).astype(BF16)


def _split_dot(x, w):
    sums = _dot(_stack_terms(x), w)
    h = x.shape[0]
    return sums[0:h] + sums[h:2 * h] + sums[2 * h:3 * h]


def _proj_kernel(x_ref, gmix_ref, *refs, tile, sub, meta):
    if meta:
        (w_ref, bf_ref, convw_ref, gconv_ref, tri_ref, perm_ref, cum0_ref, tail0_ref,
         ka_ref, vt_ref, cum_ref, tail_ref, wtok_ref, wrow_ref, cu_scr, carry_scr) = refs
        o_v, o_f = 2 * D_ATTN, 3 * D_ATTN
        wtok_ref[:, 0:D_ATTN] = w_ref[:, D_ATTN:o_v].astype(BF16)
        wtok_ref[:, D_ATTN:] = w_ref[:, o_f + N_HEADS:].astype(BF16)
        wrow_ref[0:D_ATTN] = (w_ref[:, 0:D_ATTN] * (LOG2E * HEAD_DIM ** -0.5)).T.astype(BF16)
        wrow_ref[D_ATTN:o_v] = w_ref[:, o_v:o_f].T.astype(BF16)
        lane = lax.broadcasted_iota(jnp.int32, (1, LANES), 1)
        wf = jnp.where(lane < N_HEADS, w_ref[:, o_f:o_f + LANES], 0.0).T
        wrow_ref[o_v:o_v + BF16_ROWS] = wf[0:BF16_ROWS].astype(BF16)
    else:
        (wtok_ref, wrow_ref, bf_ref, convw_ref, gconv_ref, tri_ref, perm_ref, cum0_ref, tail0_ref,
         qt_ref, ka_ref, vt_ref, yc_ref, cu_scr, carry_scr) = refs

    @pl.when(pl.program_id(1) == 0)
    def _():
        carry_scr[...] = jnp.broadcast_to(cum0_ref[:, N_META - 1:N_META], carry_scr.shape)
        cu_scr[0:SUBLANES, :] = tail0_ref[...]

    def rows_at(r0):
        xn = _rms_norm(x_ref[0, r0:r0 + sub], gmix_ref[...]).astype(BF16)

        rows = _dot_nt(wrow_ref[...], xn)
        if not meta:
            qt_ref[0, :, r0:r0 + sub] = rows[0:D_ATTN].astype(BF16)
        vt_ref[0, :, r0:r0 + sub] = rows[D_ATTN:2 * D_ATTN].astype(BF16)
        z = rows[2 * D_ATTN:2 * D_ATTN + N_HEADS] + bf_ref[...]
        log_f = jnp.minimum(z, 0.0) - jnp.log1p(jnp.exp(-jnp.abs(z)))

        c0 = D_ATTN
        cu = (_dot(xn, wtok_ref[:, c0 + D_CONV:c0 + 2 * D_CONV])
              * _dot(xn, wtok_ref[:, c0 + 2 * D_CONV:c0 + 3 * D_CONV]))
        cu_scr[SUBLANES + r0:SUBLANES + r0 + sub, :] = cu

        cum = _split_dot(log_f, tri_ref[...]) + carry_scr[:, 0:1]
        carry_scr[...] = jnp.broadcast_to(cum[:, sub - 1:sub], carry_scr.shape)
        bias = -LOG2E * cum
        if meta:
            cum_ref[...] = cum
            lane = lax.broadcasted_iota(jnp.int32, cum.shape, 1)
            bias = jnp.where(lane < N_META, bias, MASKED)
            tail_ref[...] = cu_scr[N_META:N_META + SUBLANES, :]
        else:
            cu1 = cu_scr[SUBLANES + r0 - 1:SUBLANES + r0 - 1 + sub, :]
            cu2 = cu_scr[SUBLANES + r0 - 2:SUBLANES + r0 - 2 + sub, :]
            conv = convw_ref[0:1, :] * cu2 + convw_ref[1:2, :] * cu1 + convw_ref[2:3, :] * cu
            yc = _dot(xn, wtok_ref[:, c0:c0 + D_CONV]) * conv
        k_lo = _dot(xn, wtok_ref[:, 0:2 * PAIR])
        bias_cols = _dot(perm_ref[...], _stack_terms(bias)).T
        k_hi = _dot(xn, wtok_ref[:, 2 * PAIR:4 * PAIR])
        low = lax.broadcasted_iota(jnp.int32, (1, PAIR), 1) < HEAD_DIM
        if not meta:
            yt = yc.T.reshape(D_CONV // HEAD_DIM, HEAD_DIM, sub)
            ms = jnp.mean(yt * yt, axis=1, keepdims=True)
            yn = (yt * lax.rsqrt(ms + EPS)).reshape(D_CONV, sub).T
            yc_ref[0, r0:r0 + sub] = (yn * gconv_ref[...]).astype(BF16)
        for h in range(N_HEADS):
            kp = (k_lo, k_hi)[h // 4][:, (h // 2 % 2) * PAIR:(h // 2 % 2 + 1) * PAIR]
            ka_ref[0, h, r0:r0 + sub] = jnp.where(low == (h % 2 == 0), kp, bias_cols).astype(BF16)

    for r0 in range(0, tile, sub):
        rows_at(r0)
    if not meta:
        cu_scr[0:SUBLANES, :] = cu_scr[tile:tile + SUBLANES, :]


def _projection(x, gmix, weights, bf_col, convw, gconv, cum0, tail0, *, tile, sub, meta):
    nb, seq, _ = x.shape
    nt = seq // tile
    tri = jnp.triu(jnp.ones((sub, sub), F32)).astype(BF16)
    out_lane = jnp.arange(PAIR) % HEAD_DIM
    src_row = jnp.where(out_lane < BIAS_TERMS * N_HEADS, out_lane, -1)
    perm = (src_row[:, None] == jnp.arange(4 * N_HEADS)[None, :]).astype(BF16)
    const = lambda shape: pl.BlockSpec(shape, lambda b, t: (0,) * len(shape))
    in_specs = [
        pl.BlockSpec((1, tile, D_MODEL), lambda b, t: (b, t, 0)),
        const(gmix.shape), *[const(w.shape) for w in weights], const(bf_col.shape), const(convw.shape),
        const(gconv.shape), const(tri.shape), const(perm.shape), const(cum0.shape),
        const(tail0.shape),
    ]
    tok_spec = pl.BlockSpec((1, tile, D_CONV), lambda b, t: (b, t, 0))
    row_spec = pl.BlockSpec((1, D_ATTN, tile), lambda b, t: (b, 0, t))
    ka_spec = pl.BlockSpec((1, N_HEADS, tile, PAIR), lambda b, t: (b, 0, t, 0))
    tok_shape = jax.ShapeDtypeStruct((nb, seq, D_CONV), BF16)
    row_shape = jax.ShapeDtypeStruct((nb, D_ATTN, seq), BF16)
    ka_shape = jax.ShapeDtypeStruct((nb, N_HEADS, seq, PAIR), BF16)
    if meta:
        wtok_shape = (D_MODEL, D_ATTN + 3 * D_CONV)
        wrow_shape = (2 * D_ATTN + BF16_ROWS, D_MODEL)
        out_specs = [ka_spec, row_spec, const((N_HEADS, tile)), const((SUBLANES, D_CONV)),
                     const(wtok_shape), const(wrow_shape)]
        out_shape = [ka_shape, row_shape, jax.ShapeDtypeStruct((N_HEADS, tile), F32),
                     jax.ShapeDtypeStruct((SUBLANES, D_CONV), F32),
                     jax.ShapeDtypeStruct(wtok_shape, BF16), jax.ShapeDtypeStruct(wrow_shape, BF16)]
    else:
        out_specs = [row_spec, ka_spec, row_spec, tok_spec]
        out_shape = [row_shape, ka_shape, row_shape, tok_shape]
    return pl.pallas_call(
        functools.partial(_proj_kernel, tile=tile, sub=sub, meta=meta),
        grid=(nb, nt),
        in_specs=in_specs,
        out_specs=out_specs,
        out_shape=out_shape,
        scratch_shapes=[pltpu.VMEM((tile + SUBLANES, D_CONV), F32), pltpu.VMEM((N_HEADS, LANES), F32)],
        compiler_params=pltpu.CompilerParams(
            dimension_semantics=("arbitrary", "arbitrary"), vmem_limit_bytes=48 * 1024 * 1024),
        name="proj_meta" if meta else "proj",
    )(x, gmix, *weights, bf_col, convw, gconv, tri, perm, cum0, tail0)


def _attn_kernel(qt_ref, ka_ref, vt_ref, kam_ref, vtm_ref, g_ref, wo_ref, w1_ref, w2_ref,
                 o_ref, wo_bf_ref, w1_bf_ref, w2_bf_ref,
                 m_scr, a_scr, acc_scr, s_scr, p_scr, sm_scr, pm_scr):
    tq, tk = ATTN_TQ, ATTN_TK
    assert tq == 2 * tk, "the pipeline fill assumes two diagonal key tiles per query block"
    for src, dst in ((wo_ref, wo_bf_ref), (w1_ref, w1_bf_ref), (w2_ref, w2_bf_ref)):
        dst[...] = src[...].astype(BF16)
    n_blocks = qt_ref.shape[2] // tq
    heads = range(N_HEADS)
    real_vt = vt_ref.at[0]
    sub = lax.broadcasted_iota(jnp.int32, (HEAD_DIM, tq), 0)
    ones = jnp.ones((BF16_ROWS, tk), BF16)

    def query(h, q0):
        qh = qt_ref[0, h * HEAD_DIM:(h + 1) * HEAD_DIM, q0:q0 + tq]
        ind = ((sub % N_HEADS == h) & (sub < BIAS_TERMS * N_HEADS)).astype(BF16)
        return jnp.concatenate([qh, ind] if h % 2 == 0 else [ind, qh], axis=0)

    def values(vt, h, k0, width):
        return jnp.concatenate([vt[h * HEAD_DIM:(h + 1) * HEAD_DIM, pl.ds(k0, width)], ones[:, 0:width]], axis=0)

    class Block:
        def __init__(self, i):
            self.q0, self.n_full = i * tq, i * (tq // tk)
            self.qs = [query(h, self.q0) for h in heads]
            w = i % 2
            self.s, self.p = s_scr.at[w], p_scr.at[w]
            self.m, self.a, self.acc = m_scr.at[w], a_scr.at[w], acc_scr.at[w]

        def tile_start(self, j):
            if isinstance(j, int):
                return self.q0 + (1 - j) * tk if j < 2 else (j - 2) * tk
            return pl.multiple_of(jnp.where(j < 2, self.q0 + (1 - j) * tk, (j - 2) * tk), tk)

    def scores(b, h, keys, s_buf, width, diag_offset=None):
        lo = 0 if diag_offset is None else diag_offset
        s = _dot(keys, b.qs[h][:, lo:tq])
        if diag_offset is not None:
            key = lax.broadcasted_iota(jnp.int32, s.shape, 0)
            s = jnp.where(key <= lax.broadcasted_iota(jnp.int32, s.shape, 1), s, MASKED)
        s_buf[h, 0:width, lo:tq] = s

    def softmax(b, h, s_buf, p_buf, width, first=False, lo=0):
        m_new = jnp.max(s_buf[h, 0:width, lo:tq], axis=0, keepdims=True)
        if not first:
            m = b.m[h, :, lo:tq]
            m_new = jnp.maximum(m, m_new)
            b.a[h, :, lo:tq] = jnp.exp2(m - m_new)
        b.m[h, :, lo:tq] = m_new
        p_buf[h, 0:width, lo:tq] = jnp.exp2(s_buf[h, 0:width, lo:tq] - m_new).astype(BF16)
        if lo:
            b.a[h, :, 0:lo] = jnp.ones((1, lo), F32)
            p_buf[h, 0:width, 0:lo] = jnp.zeros((width, lo), BF16)

    def weighted(b, h, vt, p_buf, k0, width, first=False):
        pv = _dot(values(vt, h, k0, width), p_buf[h, 0:width])
        b.acc[h] = pv if first else b.a[h] * b.acc[h] + pv

    def real_keys(h, k0):
        return ka_ref[0, h, pl.ds(k0, tk), :]

    def fill_scores(b, h):
        scores(b, h, kam_ref[h, 0:N_META, :], sm_scr, N_META)
        scores(b, h, real_keys(h, b.q0 + tk), b.s, tk, diag_offset=tk)

    def fill_meta(b, h):
        softmax(b, h, sm_scr, pm_scr, N_META, first=True)

    def fill_rest(b, h):
        weighted(b, h, vtm_ref, pm_scr, 0, N_META, first=True)
        softmax(b, h, b.s, b.p, tk, lo=tk)
        scores(b, h, real_keys(h, b.q0), b.s, tk, diag_offset=0)

    def trip(b, c):
        k_pv, k_qk = b.tile_start(c), pl.multiple_of(c * tk, tk)
        for h in heads:
            weighted(b, h, real_vt, b.p, k_pv, tk)
            softmax(b, h, b.s, b.p, tk)
            scores(b, h, real_keys(h, k_qk), b.s, tk)

    def drain_softmax(b, h):
        weighted(b, h, real_vt, b.p, b.tile_start(b.n_full), tk)
        softmax(b, h, b.s, b.p, tk)

    def drain_values(b, h):
        weighted(b, h, real_vt, b.p, b.tile_start(b.n_full + 1), tk)

    def finish(b):
        def normed(h):
            acc = b.acc[h]
            o = acc[0:HEAD_DIM] / acc[HEAD_DIM:HEAD_DIM + 1]
            return o * lax.rsqrt(jnp.mean(o * o, axis=0, keepdims=True) + EPS)

        for j in range(N_PAIRS):
            y = jnp.concatenate([normed(2 * j), normed(2 * j + 1)], axis=0).T
            o_ref[0, b.q0:b.q0 + tq, j * PAIR:(j + 1) * PAIR] = (
                y * g_ref[:, j * PAIR:(j + 1) * PAIR]).astype(BF16)

    blk = Block(0)
    for h in heads:
        fill_scores(blk, h)
    for h in heads:
        fill_meta(blk, h)
    for h in heads:
        fill_rest(blk, h)
    for i in range(n_blocks):

        def body(c, carry, b=blk):
            for u in range(tq // tk):
                trip(b, (tq // tk) * c + u)
            return carry

        lax.fori_loop(0, jnp.minimum(pl.program_id(0) + i, i), body, 0)
        nxt = Block(i + 1) if i + 1 < n_blocks else None
        for h in heads:
            drain_softmax(blk, h)
            if nxt:
                fill_scores(nxt, h)
        if nxt:
            for h in heads:
                fill_meta(nxt, h)
        for h in heads:
            drain_values(blk, h)
            if nxt:
                fill_rest(nxt, h)
        finish(blk)
        blk = nxt


def _attention(qt, ka, vt, kam, vtm, gattn, mlp_weights):
    nb, _, seq = qt.shape
    tq, tk = ATTN_TQ, ATTN_TK
    whole = lambda a: pl.BlockSpec(a.shape, lambda b: (0,) * a.ndim)
    per_batch = lambda a: pl.BlockSpec((1,) + a.shape[1:], lambda b: (b,) + (0,) * (a.ndim - 1))
    assert all(w.shape[0] % (nb * BF16_ROWS) == 0 for w in mlp_weights)
    row_slice = lambda w: pl.BlockSpec((w.shape[0] // nb, w.shape[1]), lambda b: (b, 0))
    return pl.pallas_call(
        _attn_kernel,
        grid=(nb,),
        in_specs=[per_batch(qt), per_batch(ka), per_batch(vt), whole(kam), whole(vtm), whole(gattn)]
        + [row_slice(w) for w in mlp_weights],
        out_specs=[pl.BlockSpec((1, seq, D_ATTN), lambda b: (b, 0, 0))] + [row_slice(w) for w in mlp_weights],
        out_shape=[jax.ShapeDtypeStruct((nb, seq, D_ATTN), BF16)]
        + [jax.ShapeDtypeStruct(w.shape, BF16) for w in mlp_weights],
        scratch_shapes=[
            pltpu.VMEM((2, N_HEADS, 1, tq), F32), pltpu.VMEM((2, N_HEADS, 1, tq), F32),
            pltpu.VMEM((2, N_HEADS, HEAD_DIM + BF16_ROWS, tq), F32),
            pltpu.VMEM((2, N_HEADS, tk, tq), F32), pltpu.VMEM((2, N_HEADS, tk, tq), BF16),
            pltpu.VMEM((N_HEADS, N_META, tq), F32), pltpu.VMEM((N_HEADS, N_META, tq), BF16),
        ],
        compiler_params=pltpu.CompilerParams(
            dimension_semantics=("arbitrary",), vmem_limit_bytes=56 * 1024 * 1024),
        name="attn",
    )(qt, ka, vt, kam, vtm, gattn, *mlp_weights)


def _mlp_kernel(x_ref, ya_ref, yc_ref, wo_ref, gmlp_ref, w1_ref, w2_ref, gfin_ref, o_ref):
    h = (x_ref[0] + _dot(ya_ref[0], wo_ref[0:D_ATTN, :]) + _dot(yc_ref[0], wo_ref[D_ATTN:D_ATTN + D_CONV, :]))
    hn = _rms_norm(h, gmlp_ref[...]).astype(BF16)
    ff = None
    for c in range(0, D_FF, FF_CHUNK):
        a = jnp.square(jnp.maximum(_dot(hn, w1_ref[:, c:c + FF_CHUNK]), 0.0)).astype(BF16)
        part = _dot(a, w2_ref[c:c + FF_CHUNK, :])
        ff = part if ff is None else ff + part
    o_ref[0] = _rms_norm(h + ff, gfin_ref[...])


def _mlp(x, ya, yc, wo, gmlp, w1, w2, gfin):
    nb, seq, _ = x.shape
    t = MLP_TILE
    const = lambda shape: pl.BlockSpec(shape, lambda b, i: (0,) * len(shape), pipeline_mode=pl.Buffered(1))
    return pl.pallas_call(
        _mlp_kernel,
        grid=(nb, seq // t),
        in_specs=[
            pl.BlockSpec((1, t, D_MODEL), lambda b, i: (b, i, 0)),
            pl.BlockSpec((1, t, D_ATTN), lambda b, i: (b, i, 0)),
            pl.BlockSpec((1, t, D_CONV), lambda b, i: (b, i, 0)),
            const(wo.shape), const(gmlp.shape), const(w1.shape), const(w2.shape), const(gfin.shape),
        ],
        out_specs=pl.BlockSpec((1, t, D_MODEL), lambda b, i: (b, i, 0)),
        out_shape=jax.ShapeDtypeStruct((nb, seq, D_MODEL), F32),
        compiler_params=pltpu.CompilerParams(
            dimension_semantics=("parallel", "parallel"), vmem_limit_bytes=56 * 1024 * 1024),
        name="mlp",
    )(x, ya, yc, wo, gmlp, w1, w2, gfin)


def kernel(x, meta_tokens, norm_mix_g, w_in, b_f, conv_w, out_norm_g, w_out, norm_mlp_g, w_ff1, w_ff2,
           final_norm_g):
    assert w_in.shape[0] == 1, "single-layer block"
    gmix = norm_mix_g[0][None, :]
    bf_col = b_f[0][:, None]
    convw = jnp.concatenate([conv_w[0], jnp.zeros((SUBLANES - CONV_WIDTH, D_CONV), F32)], axis=0)
    gattn = out_norm_g[0, :D_ATTN][None, :]
    gconv = out_norm_g[0, D_ATTN:][None, :]

    meta = jnp.zeros((1, META_PAD, D_MODEL), F32).at[0, :N_META].set(meta_tokens)
    proj = functools.partial(_projection, gmix=gmix, bf_col=bf_col, convw=convw, gconv=gconv)
    kam, vtm, cumm, tailm, wtok, wrow = proj(
        meta, weights=(w_in[0],), cum0=jnp.zeros((N_HEADS, LANES), F32),
        tail0=jnp.zeros((SUBLANES, D_CONV), F32), tile=META_PAD, sub=META_PAD, meta=True)
    qt, ka, vt, yc = proj(x, weights=(wtok, wrow), cum0=cumm, tail0=tailm, tile=PROJ_TILE, sub=PROJ_SUB,
                          meta=False)

    ya, wo, w1, w2 = _attention(qt, ka, vt, kam[0], vtm[0], gattn, (w_out[0], w_ff1[0], w_ff2[0]))

    return _mlp(x, ya, yc, wo, norm_mlp_g[0][None, :], w1, w2, final_norm_g[None, :])
```

```python
import functools

import jax
import jax.numpy as jnp
from jax import lax
from jax.experimental import pallas as pl
from jax.experimental.pallas import tpu as pltpu

D_MODEL = 1024
N_META = 16
HEAD_DIM = 64
D_ATTN = 512
N_HEADS = D_ATTN // HEAD_DIM
D_CONV = 512
CONV_WIDTH = 3
D_FF = 4096
EPS = 1e-5
MASKED = -1e30
LOG2E = 1.4426950408889634
BIAS_TERMS = 3

LANES = 128
SUBLANES = 8
BF16_ROWS = 16
PAIR = 2 * HEAD_DIM
N_PAIRS = N_HEADS // 2
META_PAD = LANES

PROJ_TILE = 1024
PROJ_SUB = 512
ATTN_TQ = 512
ATTN_TK = 256
MLP_TILE = 1024
FF_CHUNK = 512

BF16 = jnp.bfloat16
F32 = jnp.float32


def _dot(a, b):
    return jnp.dot(a, b, preferred_element_type=F32)


def _dot_nt(a, b):
    return lax.dot_general(a, b, (((1,), (1,)), ((), ())), preferred_element_type=F32)


def _rms_norm(x, g):
    ms = jnp.mean(x * x, axis=-1, keepdims=True)
    return x * lax.rsqrt(ms + EPS) * g


def _split_bf16(x, parts):
    out = []
    r = x
    for _ in range(parts):
        t = r.astype(BF16)
        out.append(t)
        r = r - t.astype(F32)
    return out


def _stack_terms(x):
    terms = [t.astype(F32) for t in _split_bf16(x, BIAS_TERMS)]
    return jnp.concatenate(terms + [jnp.zeros_like(x)], axis=0).astype(BF16)


def _split_dot(x, w):
    sums = _dot(_stack_terms(x), w)
    h = x.shape[0]
    return sums[0:h] + sums[h:2 * h] + sums[2 * h:3 * h]


def _proj_kernel(x_ref, gmix_ref, wtok_ref, wrow_ref, bf_ref, convw_ref, gconv_ref, tri_ref, perm_ref,
                 cum0_ref, tail0_ref, *refs, tile, sub, meta):
    if meta:
        ka_ref, vt_ref, cum_ref, tail_ref, cu_scr, carry_scr = refs
    else:
        qt_ref, ka_ref, vt_ref, yc_ref, cu_scr, carry_scr = refs

    @pl.when(pl.program_id(1) == 0)
    def _():
        carry_scr[...] = jnp.broadcast_to(cum0_ref[:, N_META - 1:N_META], carry_scr.shape)
        cu_scr[0:SUBLANES, :] = tail0_ref[...]

    def rows_at(r0):
        xn = _rms_norm(x_ref[0, r0:r0 + sub], gmix_ref[...]).astype(BF16)

        rows = _dot_nt(wrow_ref[...], xn)
        if not meta:
            qt_ref[0, :, r0:r0 + sub] = rows[0:D_ATTN].astype(BF16)
        vt_ref[0, :, r0:r0 + sub] = rows[D_ATTN:2 * D_ATTN].astype(BF16)
        z = rows[2 * D_ATTN:2 * D_ATTN + N_HEADS] + bf_ref[...]
        log_f = jnp.minimum(z, 0.0) - jnp.log1p(jnp.exp(-jnp.abs(z)))

        c0 = D_ATTN
        cu = (_dot(xn, wtok_ref[:, c0 + D_CONV:c0 + 2 * D_CONV])
              * _dot(xn, wtok_ref[:, c0 + 2 * D_CONV:c0 + 3 * D_CONV]))
        cu_scr[SUBLANES + r0:SUBLANES + r0 + sub, :] = cu

        cum = _split_dot(log_f, tri_ref[...]) + carry_scr[:, 0:1]
        carry_scr[...] = jnp.broadcast_to(cum[:, sub - 1:sub], carry_scr.shape)
        bias = -LOG2E * cum
        if meta:
            cum_ref[...] = cum
            lane = lax.broadcasted_iota(jnp.int32, cum.shape, 1)
            bias = jnp.where(lane < N_META, bias, MASKED)
            tail_ref[...] = cu_scr[N_META:N_META + SUBLANES, :]
        else:
            cu1 = cu_scr[SUBLANES + r0 - 1:SUBLANES + r0 - 1 + sub, :]
            cu2 = cu_scr[SUBLANES + r0 - 2:SUBLANES + r0 - 2 + sub, :]
            conv = convw_ref[0:1, :] * cu2 + convw_ref[1:2, :] * cu1 + convw_ref[2:3, :] * cu
            yc = _dot(xn, wtok_ref[:, c0:c0 + D_CONV]) * conv
        k_lo = _dot(xn, wtok_ref[:, 0:2 * PAIR])
        bias_cols = _dot(perm_ref[...], _stack_terms(bias)).T
        k_hi = _dot(xn, wtok_ref[:, 2 * PAIR:4 * PAIR])
        low = lax.broadcasted_iota(jnp.int32, (1, PAIR), 1) < HEAD_DIM
        if not meta:
            yt = yc.T.reshape(D_CONV // HEAD_DIM, HEAD_DIM, sub)
            ms = jnp.mean(yt * yt, axis=1, keepdims=True)
            yn = (yt * lax.rsqrt(ms + EPS)).reshape(D_CONV, sub).T
            yc_ref[0, r0:r0 + sub] = (yn * gconv_ref[...]).astype(BF16)
        for h in range(N_HEADS):
            kp = (k_lo, k_hi)[h // 4][:, (h // 2 % 2) * PAIR:(h // 2 % 2 + 1) * PAIR]
            ka_ref[0, h, r0:r0 + sub] = jnp.where(low == (h % 2 == 0), kp, bias_cols).astype(BF16)

    for r0 in range(0, tile, sub):
        rows_at(r0)
    if not meta:
        cu_scr[0:SUBLANES, :] = cu_scr[tile:tile + SUBLANES, :]


def _projection(x, gmix, wtok, wrow, bf_col, convw, gconv, cum0, tail0, *, tile, sub, meta):
    nb, seq, _ = x.shape
    nt = seq // tile
    tri = jnp.triu(jnp.ones((sub, sub), F32)).astype(BF16)
    out_lane = jnp.arange(PAIR) % HEAD_DIM
    src_row = jnp.where(out_lane < BIAS_TERMS * N_HEADS, out_lane, -1)
    perm = (src_row[:, None] == jnp.arange(4 * N_HEADS)[None, :]).astype(BF16)
    const = lambda shape: pl.BlockSpec(shape, lambda b, t: (0,) * len(shape))
    in_specs = [
        pl.BlockSpec((1, tile, D_MODEL), lambda b, t: (b, t, 0)),
        const(gmix.shape), const(wtok.shape), const(wrow.shape), const(bf_col.shape), const(convw.shape),
        const(gconv.shape), const(tri.shape), const(perm.shape), const(cum0.shape),
        const(tail0.shape),
    ]
    tok_spec = pl.BlockSpec((1, tile, D_CONV), lambda b, t: (b, t, 0))
    row_spec = pl.BlockSpec((1, D_ATTN, tile), lambda b, t: (b, 0, t))
    ka_spec = pl.BlockSpec((1, N_HEADS, tile, PAIR), lambda b, t: (b, 0, t, 0))
    tok_shape = jax.ShapeDtypeStruct((nb, seq, D_CONV), BF16)
    row_shape = jax.ShapeDtypeStruct((nb, D_ATTN, seq), BF16)
    ka_shape = jax.ShapeDtypeStruct((nb, N_HEADS, seq, PAIR), BF16)
    if meta:
        out_specs = [ka_spec, row_spec, const((N_HEADS, tile)), const((SUBLANES, D_CONV))]
        out_shape = [ka_shape, row_shape, jax.ShapeDtypeStruct((N_HEADS, tile), F32),
                     jax.ShapeDtypeStruct((SUBLANES, D_CONV), F32)]
    else:
        out_specs = [row_spec, ka_spec, row_spec, tok_spec]
        out_shape = [row_shape, ka_shape, row_shape, tok_shape]
    return pl.pallas_call(
        functools.partial(_proj_kernel, tile=tile, sub=sub, meta=meta),
        grid=(nb, nt),
        in_specs=in_specs,
        out_specs=out_specs,
        out_shape=out_shape,
        scratch_shapes=[pltpu.VMEM((tile + SUBLANES, D_CONV), F32), pltpu.VMEM((N_HEADS, LANES), F32)],
        compiler_params=pltpu.CompilerParams(
            dimension_semantics=("arbitrary", "arbitrary"), vmem_limit_bytes=48 * 1024 * 1024),
        name="proj_meta" if meta else "proj",
    )(x, gmix, wtok, wrow, bf_col, convw, gconv, tri, perm, cum0, tail0)


def _attn_kernel(qt_ref, ka_ref, vt_ref, kam_ref, vtm_ref, g_ref, wo_ref, w1_ref, w2_ref,
                 o_ref, wo_bf_ref, w1_bf_ref, w2_bf_ref,
                 m_scr, a_scr, acc_scr, s_scr, p_scr, sm_scr, pm_scr):
    tq, tk = ATTN_TQ, ATTN_TK
    assert tq == 2 * tk, "the pipeline fill assumes two diagonal key tiles per query block"
    for src, dst in ((wo_ref, wo_bf_ref), (w1_ref, w1_bf_ref), (w2_ref, w2_bf_ref)):
        dst[...] = src[...].astype(BF16)
    n_blocks = qt_ref.shape[2] // tq
    heads = range(N_HEADS)
    real_vt = vt_ref.at[0]
    sub = lax.broadcasted_iota(jnp.int32, (HEAD_DIM, tq), 0)
    ones = jnp.ones((BF16_ROWS, tk), BF16)

    def query(h, q0):
        qh = qt_ref[0, h * HEAD_DIM:(h + 1) * HEAD_DIM, q0:q0 + tq]
        ind = ((sub % N_HEADS == h) & (sub < BIAS_TERMS * N_HEADS)).astype(BF16)
        return jnp.concatenate([qh, ind] if h % 2 == 0 else [ind, qh], axis=0)

    def values(vt, h, k0, width):
        return jnp.concatenate([vt[h * HEAD_DIM:(h + 1) * HEAD_DIM, pl.ds(k0, width)], ones[:, 0:width]], axis=0)

    class Block:
        def __init__(self, i):
            self.q0, self.n_full = i * tq, i * (tq // tk)
            self.qs = [query(h, self.q0) for h in heads]
            w = i % 2
            self.s, self.p = s_scr.at[w], p_scr.at[w]
            self.m, self.a, self.acc = m_scr.at[w], a_scr.at[w], acc_scr.at[w]

        def tile_start(self, j):
            if isinstance(j, int):
                return self.q0 + (1 - j) * tk if j < 2 else (j - 2) * tk
            return pl.multiple_of(jnp.where(j < 2, self.q0 + (1 - j) * tk, (j - 2) * tk), tk)

    def scores(b, h, keys, s_buf, width, diag_offset=None):
        lo = 0 if diag_offset is None else diag_offset
        s = _dot(keys, b.qs[h][:, lo:tq])
        if diag_offset is not None:
            key = lax.broadcasted_iota(jnp.int32, s.shape, 0)
            s = jnp.where(key <= lax.broadcasted_iota(jnp.int32, s.shape, 1), s, MASKED)
        s_buf[h, 0:width, lo:tq] = s

    def softmax(b, h, s_buf, p_buf, width, first=False, lo=0):
        m_new = jnp.max(s_buf[h, 0:width, lo:tq], axis=0, keepdims=True)
        if not first:
            m = b.m[h, :, lo:tq]
            m_new = jnp.maximum(m, m_new)
            b.a[h, :, lo:tq] = jnp.exp2(m - m_new)
        b.m[h, :, lo:tq] = m_new
        p_buf[h, 0:width, lo:tq] = jnp.exp2(s_buf[h, 0:width, lo:tq] - m_new).astype(BF16)
        if lo:
            b.a[h, :, 0:lo] = jnp.ones((1, lo), F32)
            p_buf[h, 0:width, 0:lo] = jnp.zeros((width, lo), BF16)

    def weighted(b, h, vt, p_buf, k0, width, first=False):
        pv = _dot(values(vt, h, k0, width), p_buf[h, 0:width])
        b.acc[h] = pv if first else b.a[h] * b.acc[h] + pv

    def real_keys(h, k0):
        return ka_ref[0, h, pl.ds(k0, tk), :]

    def fill_scores(b, h):
        scores(b, h, kam_ref[h, 0:N_META, :], sm_scr, N_META)
        scores(b, h, real_keys(h, b.q0 + tk), b.s, tk, diag_offset=tk)

    def fill_meta(b, h):
        softmax(b, h, sm_scr, pm_scr, N_META, first=True)

    def fill_rest(b, h):
        weighted(b, h, vtm_ref, pm_scr, 0, N_META, first=True)
        softmax(b, h, b.s, b.p, tk, lo=tk)
        scores(b, h, real_keys(h, b.q0), b.s, tk, diag_offset=0)

    def trip(b, c):
        k_pv, k_qk = b.tile_start(c), pl.multiple_of(c * tk, tk)
        for h in heads:
            weighted(b, h, real_vt, b.p, k_pv, tk)
            softmax(b, h, b.s, b.p, tk)
            scores(b, h, real_keys(h, k_qk), b.s, tk)

    def drain_softmax(b, h):
        weighted(b, h, real_vt, b.p, b.tile_start(b.n_full), tk)
        softmax(b, h, b.s, b.p, tk)

    def drain_values(b, h):
        weighted(b, h, real_vt, b.p, b.tile_start(b.n_full + 1), tk)

    def finish(b):
        def normed(h):
            acc = b.acc[h]
            o = acc[0:HEAD_DIM] / acc[HEAD_DIM:HEAD_DIM + 1]
            return o * lax.rsqrt(jnp.mean(o * o, axis=0, keepdims=True) + EPS)

        for j in range(N_PAIRS):
            y = jnp.concatenate([normed(2 * j), normed(2 * j + 1)], axis=0).T
            o_ref[0, b.q0:b.q0 + tq, j * PAIR:(j + 1) * PAIR] = (
                y * g_ref[:, j * PAIR:(j + 1) * PAIR]).astype(BF16)

    blk = Block(0)
    for h in heads:
        fill_scores(blk, h)
    for h in heads:
        fill_meta(blk, h)
    for h in heads:
        fill_rest(blk, h)
    for i in range(n_blocks):

        def body(c, carry, b=blk):
            for u in range(tq // tk):
                trip(b, (tq // tk) * c + u)
            return carry

        lax.fori_loop(0, jnp.minimum(pl.program_id(0) + i, i), body, 0)
        nxt = Block(i + 1) if i + 1 < n_blocks else None
        for h in heads:
            drain_softmax(blk, h)
            if nxt:
                fill_scores(nxt, h)
        if nxt:
            for h in heads:
                fill_meta(nxt, h)
        for h in heads:
            drain_values(blk, h)
            if nxt:
                fill_rest(nxt, h)
        finish(blk)
        blk = nxt


def _attention(qt, ka, vt, kam, vtm, gattn, mlp_weights):
    nb, _, seq = qt.shape
    tq, tk = ATTN_TQ, ATTN_TK
    whole = lambda a: pl.BlockSpec(a.shape, lambda b: (0,) * a.ndim)
    per_batch = lambda a: pl.BlockSpec((1,) + a.shape[1:], lambda b: (b,) + (0,) * (a.ndim - 1))
    assert all(w.shape[0] % (nb * BF16_ROWS) == 0 for w in mlp_weights)
    row_slice = lambda w: pl.BlockSpec((w.shape[0] // nb, w.shape[1]), lambda b: (b, 0))
    return pl.pallas_call(
        _attn_kernel,
        grid=(nb,),
        in_specs=[per_batch(qt), per_batch(ka), per_batch(vt), whole(kam), whole(vtm), whole(gattn)]
        + [row_slice(w) for w in mlp_weights],
        out_specs=[pl.BlockSpec((1, seq, D_ATTN), lambda b: (b, 0, 0))] + [row_slice(w) for w in mlp_weights],
        out_shape=[jax.ShapeDtypeStruct((nb, seq, D_ATTN), BF16)]
        + [jax.ShapeDtypeStruct(w.shape, BF16) for w in mlp_weights],
        scratch_shapes=[
            pltpu.VMEM((2, N_HEADS, 1, tq), F32), pltpu.VMEM((2, N_HEADS, 1, tq), F32),
            pltpu.VMEM((2, N_HEADS, HEAD_DIM + BF16_ROWS, tq), F32),
            pltpu.VMEM((2, N_HEADS, tk, tq), F32), pltpu.VMEM((2, N_HEADS, tk, tq), BF16),
            pltpu.VMEM((N_HEADS, N_META, tq), F32), pltpu.VMEM((N_HEADS, N_META, tq), BF16),
        ],
        compiler_params=pltpu.CompilerParams(
            dimension_semantics=("arbitrary",), vmem_limit_bytes=56 * 1024 * 1024),
        name="attn",
    )(qt, ka, vt, kam, vtm, gattn, *mlp_weights)


def _mlp_kernel(x_ref, ya_ref, yc_ref, wo_ref, gmlp_ref, w1_ref, w2_ref, gfin_ref, o_ref):
    halves = [slice(r0, r0 + MLP_TILE // 2) for r0 in (0, MLP_TILE // 2)]
    hs, hns, ffs = [], [], [None, None]
    for rows in halves:
        h = (x_ref[0, rows] + _dot(ya_ref[0, rows], wo_ref[0:D_ATTN, :])
             + _dot(yc_ref[0, rows], wo_ref[D_ATTN:D_ATTN + D_CONV, :]))
        hs.append(h)
        hns.append(_rms_norm(h, gmlp_ref[...]).astype(BF16))
    for c in range(0, D_FF, FF_CHUNK):
        acts = [jnp.square(jnp.maximum(_dot(hn, w1_ref[:, c:c + FF_CHUNK]), 0.0)).astype(BF16) for hn in hns]
        for i, a in enumerate(acts):
            part = _dot(a, w2_ref[c:c + FF_CHUNK, :])
            ffs[i] = part if ffs[i] is None else ffs[i] + part
    for rows, h, ff in zip(halves, hs, ffs):
        o_ref[0, rows] = _rms_norm(h + ff, gfin_ref[...])


def _mlp(x, ya, yc, wo, gmlp, w1, w2, gfin):
    nb, seq, _ = x.shape
    t = MLP_TILE
    const = lambda shape: pl.BlockSpec(shape, lambda b, i: (0,) * len(shape), pipeline_mode=pl.Buffered(1))
    return pl.pallas_call(
        _mlp_kernel,
        grid=(nb, seq // t),
        in_specs=[
            pl.BlockSpec((1, t, D_MODEL), lambda b, i: (b, i, 0)),
            pl.BlockSpec((1, t, D_ATTN), lambda b, i: (b, i, 0)),
            pl.BlockSpec((1, t, D_CONV), lambda b, i: (b, i, 0)),
            const(wo.shape), const(gmlp.shape), const(w1.shape), const(w2.shape), const(gfin.shape),
        ],
        out_specs=pl.BlockSpec((1, t, D_MODEL), lambda b, i: (b, i, 0)),
        out_shape=jax.ShapeDtypeStruct((nb, seq, D_MODEL), F32),
        compiler_params=pltpu.CompilerParams(
            dimension_semantics=("parallel", "parallel"), vmem_limit_bytes=56 * 1024 * 1024),
        name="mlp",
    )(x, ya, yc, wo, gmlp, w1, w2, gfin)


def kernel(x, meta_tokens, norm_mix_g, w_in, b_f, conv_w, out_norm_g, w_out, norm_mlp_g, w_ff1, w_ff2,
           final_norm_g):
    assert w_in.shape[0] == 1, "single-layer block"
    w = w_in[0]
    o_q, o_k, o_v, o_f = 0, D_ATTN, 2 * D_ATTN, 3 * D_ATTN
    o_b = o_f + N_HEADS
    wq = w[:, o_q:o_k] * (LOG2E * HEAD_DIM ** -0.5)
    wtok = jnp.concatenate([w[:, o_k:o_v], w[:, o_b:]], axis=1).astype(BF16)
    wrow = jnp.concatenate([wq, w[:, o_v:o_f], w[:, o_f:o_b], jnp.zeros((D_MODEL, SUBLANES), F32)], axis=1)
    wrow = wrow.T.astype(BF16)
    gmix = norm_mix_g[0][None, :]
    bf_col = b_f[0][:, None]
    convw = jnp.concatenate([conv_w[0], jnp.zeros((SUBLANES - CONV_WIDTH, D_CONV), F32)], axis=0)
    gattn = out_norm_g[0, :D_ATTN][None, :]
    gconv = out_norm_g[0, D_ATTN:][None, :]

    meta = jnp.zeros((1, META_PAD, D_MODEL), F32).at[0, :N_META].set(meta_tokens)
    proj = functools.partial(_projection, gmix=gmix, wtok=wtok, wrow=wrow, bf_col=bf_col, convw=convw,
                             gconv=gconv)
    kam, vtm, cumm, tailm = proj(meta, cum0=jnp.zeros((N_HEADS, LANES), F32),
                                 tail0=jnp.zeros((SUBLANES, D_CONV), F32), tile=META_PAD, sub=META_PAD,
                                 meta=True)
    qt, ka, vt, yc = proj(x, cum0=cumm, tail0=tailm, tile=PROJ_TILE, sub=PROJ_SUB, meta=False)

    ya, wo, w1, w2 = _attention(qt, ka, vt, kam[0], vtm[0], gattn, (w_out[0], w_ff1[0], w_ff2[0]))

    return _mlp(x, ya, yc, wo, norm_mlp_g[0][None, :], w1, w2, final_norm_g[None, :])
```

```python
import functools

import jax
import jax.numpy as jnp
from jax import lax
from jax.experimental import pallas as pl
from jax.experimental.pallas import tpu as pltpu

D_MODEL = 1024
N_META = 16
HEAD_DIM = 64
D_ATTN = 512
N_HEADS = D_ATTN // HEAD_DIM
D_CONV = 512
CONV_WIDTH = 3
D_FF = 4096
EPS = 1e-5
MASKED = -1e30
LOG2E = 1.4426950408889634
BIAS_TERMS = 3

LANES = 128
SUBLANES = 8
BF16_ROWS = 16
PAIR = 2 * HEAD_DIM
N_PAIRS = N_HEADS // 2
META_PAD = LANES

PROJ_TILE = 1024
PROJ_SUB = 512
ATTN_TQ = 512
ATTN_TK = 256
MLP_TILE = 1024
FF_CHUNK = 512

BF16 = jnp.bfloat16
F32 = jnp.float32


def _dot(a, b):
    return jnp.dot(a, b, preferred_element_type=F32)


def _dot_nt(a, b):
    return lax.dot_general(a, b, (((1,), (1,)), ((), ())), preferred_element_type=F32)


def _rms_norm(x, g):
    ms = jnp.mean(x * x, axis=-1, keepdims=True)
    return x * lax.rsqrt(ms + EPS) * g


def _split_bf16(x, parts):
    out = []
    r = x
    for _ in range(parts):
        t = r.astype(BF16)
        out.append(t)
        r = r - t.astype(F32)
    return out


def _stack_terms(x):
    terms = [t.astype(F32) for t in _split_bf16(x, BIAS_TERMS)]
    return jnp.concatenate(terms + [jnp.zeros_like(x)], axis=0).astype(BF16)


def _split_dot(x, w):
    sums = _dot(_stack_terms(x), w)
    h = x.shape[0]
    return sums[0:h] + sums[h:2 * h] + sums[2 * h:3 * h]


def _proj_kernel(x_ref, gmix_ref, wtok_ref, wrow_ref, bf_ref, convw_ref, gconv_ref, tri_ref, perm_ref,
                 cum0_ref, tail0_ref, *refs, tile, sub, meta):
    if meta:
        ka_ref, vt_ref, cum_ref, tail_ref, cu_scr, carry_scr = refs
    else:
        qt_ref, ka_ref, vt_ref, yc_ref, cu_scr, carry_scr = refs

    @pl.when(pl.program_id(1) == 0)
    def _():
        carry_scr[...] = jnp.broadcast_to(cum0_ref[:, N_META - 1:N_META], carry_scr.shape)
        cu_scr[0:SUBLANES, :] = tail0_ref[...]

    def rows_at(r0):
        xn = _rms_norm(x_ref[0, r0:r0 + sub], gmix_ref[...]).astype(BF16)

        rows = _dot_nt(wrow_ref[...], xn)
        if not meta:
            qt_ref[0, :, r0:r0 + sub] = rows[0:D_ATTN].astype(BF16)
        vt_ref[0, :, r0:r0 + sub] = rows[D_ATTN:2 * D_ATTN].astype(BF16)
        z = rows[2 * D_ATTN:2 * D_ATTN + N_HEADS] + bf_ref[...]
        log_f = jnp.minimum(z, 0.0) - jnp.log1p(jnp.exp(-jnp.abs(z)))

        c0 = D_ATTN
        cu = (_dot(xn, wtok_ref[:, c0 + D_CONV:c0 + 2 * D_CONV])
              * _dot(xn, wtok_ref[:, c0 + 2 * D_CONV:c0 + 3 * D_CONV]))
        cu_scr[SUBLANES + r0:SUBLANES + r0 + sub, :] = cu

        cum = _split_dot(log_f, tri_ref[...]) + carry_scr[:, 0:1]
        carry_scr[...] = jnp.broadcast_to(cum[:, sub - 1:sub], carry_scr.shape)
        bias = -LOG2E * cum
        if meta:
            cum_ref[...] = cum
            lane = lax.broadcasted_iota(jnp.int32, cum.shape, 1)
            bias = jnp.where(lane < N_META, bias, MASKED)
            tail_ref[...] = cu_scr[N_META:N_META + SUBLANES, :]
        else:
            cu1 = cu_scr[SUBLANES + r0 - 1:SUBLANES + r0 - 1 + sub, :]
            cu2 = cu_scr[SUBLANES + r0 - 2:SUBLANES + r0 - 2 + sub, :]
            conv = convw_ref[0:1, :] * cu2 + convw_ref[1:2, :] * cu1 + convw_ref[2:3, :] * cu
            yc = _dot(xn, wtok_ref[:, c0:c0 + D_CONV]) * conv
        k_lo = _dot(xn, wtok_ref[:, 0:2 * PAIR])
        bias_cols = _dot(perm_ref[...], _stack_terms(bias)).T
        k_hi = _dot(xn, wtok_ref[:, 2 * PAIR:4 * PAIR])
        low = lax.broadcasted_iota(jnp.int32, (1, PAIR), 1) < HEAD_DIM
        if not meta:
            yt = yc.T.reshape(D_CONV // HEAD_DIM, HEAD_DIM, sub)
            ms = jnp.mean(yt * yt, axis=1, keepdims=True)
            yn = (yt * lax.rsqrt(ms + EPS)).reshape(D_CONV, sub).T
            yc_ref[0, r0:r0 + sub] = (yn * gconv_ref[...]).astype(BF16)
        for h in range(N_HEADS):
            kp = (k_lo, k_hi)[h // 4][:, (h // 2 % 2) * PAIR:(h // 2 % 2 + 1) * PAIR]
            ka_ref[0, h, r0:r0 + sub] = jnp.where(low == (h % 2 == 0), kp, bias_cols).astype(BF16)

    for r0 in range(0, tile, sub):
        rows_at(r0)
    if not meta:
        cu_scr[0:SUBLANES, :] = cu_scr[tile:tile + SUBLANES, :]


def _projection(x, gmix, wtok, wrow, bf_col, convw, gconv, cum0, tail0, *, tile, sub, meta):
    nb, seq, _ = x.shape
    nt = seq // tile
    tri = jnp.triu(jnp.ones((sub, sub), F32)).astype(BF16)
    out_lane = jnp.arange(PAIR) % HEAD_DIM
    src_row = jnp.where(out_lane < BIAS_TERMS * N_HEADS, out_lane, -1)
    perm = (src_row[:, None] == jnp.arange(4 * N_HEADS)[None, :]).astype(BF16)
    const = lambda shape: pl.BlockSpec(shape, lambda b, t: (0,) * len(shape))
    in_specs = [
        pl.BlockSpec((1, tile, D_MODEL), lambda b, t: (b, t, 0)),
        const(gmix.shape), const(wtok.shape), const(wrow.shape), const(bf_col.shape), const(convw.shape),
        const(gconv.shape), const(tri.shape), const(perm.shape), const(cum0.shape),
        const(tail0.shape),
    ]
    tok_spec = pl.BlockSpec((1, tile, D_CONV), lambda b, t: (b, t, 0))
    row_spec = pl.BlockSpec((1, D_ATTN, tile), lambda b, t: (b, 0, t))
    ka_spec = pl.BlockSpec((1, N_HEADS, tile, PAIR), lambda b, t: (b, 0, t, 0))
    tok_shape = jax.ShapeDtypeStruct((nb, seq, D_CONV), BF16)
    row_shape = jax.ShapeDtypeStruct((nb, D_ATTN, seq), BF16)
    ka_shape = jax.ShapeDtypeStruct((nb, N_HEADS, seq, PAIR), BF16)
    if meta:
        out_specs = [ka_spec, row_spec, const((N_HEADS, tile)), const((SUBLANES, D_CONV))]
        out_shape = [ka_shape, row_shape, jax.ShapeDtypeStruct((N_HEADS, tile), F32),
                     jax.ShapeDtypeStruct((SUBLANES, D_CONV), F32)]
    else:
        out_specs = [row_spec, ka_spec, row_spec, tok_spec]
        out_shape = [row_shape, ka_shape, row_shape, tok_shape]
    return pl.pallas_call(
        functools.partial(_proj_kernel, tile=tile, sub=sub, meta=meta),
        grid=(nb, nt),
        in_specs=in_specs,
        out_specs=out_specs,
        out_shape=out_shape,
        scratch_shapes=[pltpu.VMEM((tile + SUBLANES, D_CONV), F32), pltpu.VMEM((N_HEADS, LANES), F32)],
        compiler_params=pltpu.CompilerParams(
            dimension_semantics=("arbitrary", "arbitrary"), vmem_limit_bytes=48 * 1024 * 1024),
        name="proj_meta" if meta else "proj",
    )(x, gmix, wtok, wrow, bf_col, convw, gconv, tri, perm, cum0, tail0)


def _attn_kernel(qt_ref, ka_ref, vt_ref, kam_ref, vtm_ref, g_ref, wo_ref, w1_ref, w2_ref,
                 o_ref, wo_bf_ref, w1_bf_ref, w2_bf_ref,
                 m_scr, a_scr, acc_scr, s_scr, p_scr, sm_scr, pm_scr):
    tq, tk = ATTN_TQ, ATTN_TK
    assert tq == 2 * tk, "the pipeline fill assumes two diagonal key tiles per query block"
    n_blocks = qt_ref.shape[2] // tq
    heads = range(N_HEADS)
    real_vt = vt_ref.at[0]
    sub = lax.broadcasted_iota(jnp.int32, (HEAD_DIM, tq), 0)
    ones = jnp.ones((BF16_ROWS, tk), BF16)

    def query(h, q0):
        qh = qt_ref[0, h * HEAD_DIM:(h + 1) * HEAD_DIM, q0:q0 + tq]
        ind = ((sub % N_HEADS == h) & (sub < BIAS_TERMS * N_HEADS)).astype(BF16)
        return jnp.concatenate([qh, ind] if h % 2 == 0 else [ind, qh], axis=0)

    def values(vt, h, k0, width):
        return jnp.concatenate([vt[h * HEAD_DIM:(h + 1) * HEAD_DIM, pl.ds(k0, width)], ones[:, 0:width]], axis=0)

    class Block:
        def __init__(self, i):
            self.q0, self.n_full = i * tq, i * (tq // tk)
            self.qs = [query(h, self.q0) for h in heads]
            w = i % 2
            self.s, self.p = s_scr.at[w], p_scr.at[w]
            self.m, self.a, self.acc = m_scr.at[w], a_scr.at[w], acc_scr.at[w]

        def tile_start(self, j):
            if isinstance(j, int):
                return self.q0 + (1 - j) * tk if j < 2 else (j - 2) * tk
            return pl.multiple_of(jnp.where(j < 2, self.q0 + (1 - j) * tk, (j - 2) * tk), tk)

    def scores(b, h, keys, s_buf, width, diag_offset=None):
        lo = 0 if diag_offset is None else diag_offset
        s = _dot(keys, b.qs[h][:, lo:tq])
        if diag_offset is not None:
            key = lax.broadcasted_iota(jnp.int32, s.shape, 0)
            s = jnp.where(key <= lax.broadcasted_iota(jnp.int32, s.shape, 1), s, MASKED)
        s_buf[h, 0:width, lo:tq] = s

    def softmax(b, h, s_buf, p_buf, width, first=False, lo=0):
        m_new = jnp.max(s_buf[h, 0:width, lo:tq], axis=0, keepdims=True)
        if not first:
            m = b.m[h, :, lo:tq]
            m_new = jnp.maximum(m, m_new)
            b.a[h, :, lo:tq] = jnp.exp2(m - m_new)
        b.m[h, :, lo:tq] = m_new
        p_buf[h, 0:width, lo:tq] = jnp.exp2(s_buf[h, 0:width, lo:tq] - m_new).astype(BF16)
        if lo:
            b.a[h, :, 0:lo] = jnp.ones((1, lo), F32)
            p_buf[h, 0:width, 0:lo] = jnp.zeros((width, lo), BF16)

    def weighted(b, h, vt, p_buf, k0, width, first=False):
        pv = _dot(values(vt, h, k0, width), p_buf[h, 0:width])
        b.acc[h] = pv if first else b.a[h] * b.acc[h] + pv

    def real_keys(h, k0):
        return ka_ref[0, h, pl.ds(k0, tk), :]

    def fill_scores(b, h):
        scores(b, h, kam_ref[h, 0:N_META, :], sm_scr, N_META)
        scores(b, h, real_keys(h, b.q0 + tk), b.s, tk, diag_offset=tk)

    def fill_meta(b, h):
        softmax(b, h, sm_scr, pm_scr, N_META, first=True)

    def fill_rest(b, h):
        weighted(b, h, vtm_ref, pm_scr, 0, N_META, first=True)
        softmax(b, h, b.s, b.p, tk, lo=tk)
        scores(b, h, real_keys(h, b.q0), b.s, tk, diag_offset=0)

    def trip(b, c):
        k_pv, k_qk = b.tile_start(c), pl.multiple_of(c * tk, tk)
        for h in heads:
            weighted(b, h, real_vt, b.p, k_pv, tk)
            softmax(b, h, b.s, b.p, tk)
            scores(b, h, real_keys(h, k_qk), b.s, tk)

    def drain_softmax(b, h):
        weighted(b, h, real_vt, b.p, b.tile_start(b.n_full), tk)
        softmax(b, h, b.s, b.p, tk)

    def drain_values(b, h):
        weighted(b, h, real_vt, b.p, b.tile_start(b.n_full + 1), tk)

    def finish(b):
        def normed(h):
            acc = b.acc[h]
            o = acc[0:HEAD_DIM] / acc[HEAD_DIM:HEAD_DIM + 1]
            return o * lax.rsqrt(jnp.mean(o * o, axis=0, keepdims=True) + EPS)

        for j in range(N_PAIRS):
            y = jnp.concatenate([normed(2 * j), normed(2 * j + 1)], axis=0).T
            o_ref[0, b.q0:b.q0 + tq, j * PAIR:(j + 1) * PAIR] = (
                y * g_ref[:, j * PAIR:(j + 1) * PAIR]).astype(BF16)

    blk = Block(0)
    for h in heads:
        fill_scores(blk, h)
    for h in heads:
        fill_meta(blk, h)
    for h in heads:
        fill_rest(blk, h)
    for i in range(n_blocks):

        def body(c, carry, b=blk):
            trip(b, c)
            return carry

        n_trips = i * (tq // tk)
        lax.fori_loop(0, jnp.minimum(pl.program_id(0) + n_trips, n_trips), body, 0)
        nxt = Block(i + 1) if i + 1 < n_blocks else None
        for h in heads:
            drain_softmax(blk, h)
            if nxt:
                fill_scores(nxt, h)
        if nxt:
            for h in heads:
                fill_meta(nxt, h)
        for h in heads:
            drain_values(blk, h)
            if nxt:
                fill_rest(nxt, h)
        finish(blk)
        blk = nxt
    for src, dst in ((wo_ref, wo_bf_ref), (w1_ref, w1_bf_ref), (w2_ref, w2_bf_ref)):
        dst[...] = src[...].astype(BF16)


def _attention(qt, ka, vt, kam, vtm, gattn, mlp_weights):
    nb, _, seq = qt.shape
    tq, tk = ATTN_TQ, ATTN_TK
    whole = lambda a: pl.BlockSpec(a.shape, lambda b: (0,) * a.ndim)
    per_batch = lambda a: pl.BlockSpec((1,) + a.shape[1:], lambda b: (b,) + (0,) * (a.ndim - 1))
    assert all(w.shape[0] % (nb * BF16_ROWS) == 0 for w in mlp_weights)
    row_slice = lambda w: pl.BlockSpec((w.shape[0] // nb, w.shape[1]), lambda b: (b, 0))
    return pl.pallas_call(
        _attn_kernel,
        grid=(nb,),
        in_specs=[per_batch(qt), per_batch(ka), per_batch(vt), whole(kam), whole(vtm), whole(gattn)]
        + [row_slice(w) for w in mlp_weights],
        out_specs=[pl.BlockSpec((1, seq, D_ATTN), lambda b: (b, 0, 0))] + [row_slice(w) for w in mlp_weights],
        out_shape=[jax.ShapeDtypeStruct((nb, seq, D_ATTN), BF16)]
        + [jax.ShapeDtypeStruct(w.shape, BF16) for w in mlp_weights],
        scratch_shapes=[
            pltpu.VMEM((2, N_HEADS, 1, tq), F32), pltpu.VMEM((2, N_HEADS, 1, tq), F32),
            pltpu.VMEM((2, N_HEADS, HEAD_DIM + BF16_ROWS, tq), F32),
            pltpu.VMEM((2, N_HEADS, tk, tq), F32), pltpu.VMEM((2, N_HEADS, tk, tq), BF16),
            pltpu.VMEM((N_HEADS, N_META, tq), F32), pltpu.VMEM((N_HEADS, N_META, tq), BF16),
        ],
        compiler_params=pltpu.CompilerParams(
            dimension_semantics=("arbitrary",), vmem_limit_bytes=56 * 1024 * 1024),
        name="attn",
    )(qt, ka, vt, kam, vtm, gattn, *mlp_weights)


def _mlp_kernel(x_ref, ya_ref, yc_ref, wo_ref, gmlp_ref, w1_ref, w2_ref, gfin_ref, o_ref):
    halves = [slice(r0, r0 + MLP_TILE // 2) for r0 in (0, MLP_TILE // 2)]
    hs, hns, ffs = [], [], [None, None]
    for rows in halves:
        h = (x_ref[0, rows] + _dot(ya_ref[0, rows], wo_ref[0:D_ATTN, :])
             + _dot(yc_ref[0, rows], wo_ref[D_ATTN:D_ATTN + D_CONV, :]))
        hs.append(h)
        hns.append(_rms_norm(h, gmlp_ref[...]).astype(BF16))
    for c in range(0, D_FF, FF_CHUNK):
        acts = [jnp.square(jnp.maximum(_dot(hn, w1_ref[:, c:c + FF_CHUNK]), 0.0)).astype(BF16) for hn in hns]
        for i, a in enumerate(acts):
            part = _dot(a, w2_ref[c:c + FF_CHUNK, :])
            ffs[i] = part if ffs[i] is None else ffs[i] + part
    for rows, h, ff in zip(halves, hs, ffs):
        o_ref[0, rows] = _rms_norm(h + ff, gfin_ref[...])


def _mlp(x, ya, yc, wo, gmlp, w1, w2, gfin):
    nb, seq, _ = x.shape
    t = MLP_TILE
    const = lambda shape: pl.BlockSpec(shape, lambda b, i: (0,) * len(shape), pipeline_mode=pl.Buffered(1))
    return pl.pallas_call(
        _mlp_kernel,
        grid=(nb, seq // t),
        in_specs=[
            pl.BlockSpec((1, t, D_MODEL), lambda b, i: (b, i, 0)),
            pl.BlockSpec((1, t, D_ATTN), lambda b, i: (b, i, 0)),
            pl.BlockSpec((1, t, D_CONV), lambda b, i: (b, i, 0)),
            const(wo.shape), const(gmlp.shape), const(w1.shape), const(w2.shape), const(gfin.shape),
        ],
        out_specs=pl.BlockSpec((1, t, D_MODEL), lambda b, i: (b, i, 0)),
        out_shape=jax.ShapeDtypeStruct((nb, seq, D_MODEL), F32),
        compiler_params=pltpu.CompilerParams(
            dimension_semantics=("parallel", "parallel"), vmem_limit_bytes=56 * 1024 * 1024),
        name="mlp",
    )(x, ya, yc, wo, gmlp, w1, w2, gfin)


def kernel(x, meta_tokens, norm_mix_g, w_in, b_f, conv_w, out_norm_g, w_out, norm_mlp_g, w_ff1, w_ff2,
           final_norm_g):
    assert w_in.shape[0] == 1, "single-layer block"
    w = w_in[0]
    o_q, o_k, o_v, o_f = 0, D_ATTN, 2 * D_ATTN, 3 * D_ATTN
    o_b = o_f + N_HEADS
    wq = w[:, o_q:o_k] * (LOG2E * HEAD_DIM ** -0.5)
    wtok = jnp.concatenate([w[:, o_k:o_v], w[:, o_b:]], axis=1).astype(BF16)
    wrow = jnp.concatenate([wq, w[:, o_v:o_f], w[:, o_f:o_b], jnp.zeros((D_MODEL, SUBLANES), F32)], axis=1)
    wrow = wrow.T.astype(BF16)
    gmix = norm_mix_g[0][None, :]
    bf_col = b_f[0][:, None]
    convw = jnp.concatenate([conv_w[0], jnp.zeros((SUBLANES - CONV_WIDTH, D_CONV), F32)], axis=0)
    gattn = out_norm_g[0, :D_ATTN][None, :]
    gconv = out_norm_g[0, D_ATTN:][None, :]

    meta = jnp.zeros((1, META_PAD, D_MODEL), F32).at[0, :N_META].set(meta_tokens)
    proj = functools.partial(_projection, gmix=gmix, wtok=wtok, wrow=wrow, bf_col=bf_col, convw=convw,
                             gconv=gconv)
    kam, vtm, cumm, tailm = proj(meta, cum0=jnp.zeros((N_HEADS, LANES), F32),
                                 tail0=jnp.zeros((SUBLANES, D_CONV), F32), tile=META_PAD, sub=META_PAD,
                                 meta=True)
    qt, ka, vt, yc = proj(x, cum0=cumm, tail0=tailm, tile=PROJ_TILE, sub=PROJ_SUB, meta=False)

    ya, wo, w1, w2 = _attention(qt, ka, vt, kam[0], vtm[0], gattn, (w_out[0], w_ff1[0], w_ff2[0]))

    return _mlp(x, ya, yc, wo, norm_mlp_g[0][None, :], w1, w2, final_norm_g[None, :])
```
